```python
import jax, jax.numpy as jnp
from jax import lax
import numpy as np

D_MODEL = 1024
BATCH = 4
SEQ = 8192
DEPTH = 2

HEAD_DIM = 64
N_Q_HEADS = D_MODEL // HEAD_DIM
N_KV_HEADS = 2
GQA_GROUP = N_Q_HEADS // N_KV_HEADS
WINDOW = 128
ATTN_WIDTH = N_Q_HEADS * HEAD_DIM
KV_WIDTH = N_KV_HEADS * HEAD_DIM
N_BUCKETS = 32
MAX_DISTANCE = 128
SGU_WIDTH = D_MODEL
SGU_GROUPS = 8
SGU_GROUP_DIM = SGU_WIDTH // SGU_GROUPS
CHUNK = 128
N_BRANCHES = 2
EPS = 1e-6
NEG_INF = -1e30
SPLITS = (ATTN_WIDTH, KV_WIDTH, KV_WIDTH, ATTN_WIDTH, SGU_WIDTH, SGU_WIDTH, SGU_WIDTH, D_MODEL, D_MODEL)
IN_COLS = 2 * ATTN_WIDTH + 2 * KV_WIDTH + 3 * SGU_WIDTH + N_BRANCHES * D_MODEL

kernel_name = "hybrid_swa_sink_gmlp_gated_block"


def rms_norm(x, g):
    xf = x.astype(jnp.float32)
    y = xf * lax.rsqrt(jnp.mean(xf * xf, axis=-1, keepdims=True) + EPS)
    return (y * g.astype(jnp.float32)).astype(x.dtype)


def layer_norm(x, g, b):
    xf = x.astype(jnp.float32)
    mu = jnp.mean(xf, axis=-1, keepdims=True)
    xc = xf - mu
    y = xc * lax.rsqrt(jnp.mean(xc * xc, axis=-1, keepdims=True) + EPS)
    return (y * g.astype(jnp.float32) + b.astype(jnp.float32)).astype(x.dtype)


def t5_bucket(dist):
    max_exact = N_BUCKETS // 2
    large = max_exact + (np.log(np.maximum(dist, max_exact) / max_exact)
                         / np.log(MAX_DISTANCE / max_exact)
                         * (N_BUCKETS - max_exact)).astype(np.int32)
    large = np.minimum(large, N_BUCKETS - 1)
    return np.where(dist < max_exact, dist, large).astype(np.int32)


def sliding_window_attention(q, k, v, sinks, rel_bias):
    B, S = q.shape[0], q.shape[1]
    nb = S // WINDOW
    qb = q.reshape(B, nb, WINDOW, N_KV_HEADS, GQA_GROUP, HEAD_DIM)

    def band(t):
        tb = t.reshape(B, nb, WINDOW, N_KV_HEADS, HEAD_DIM)
        prev = jnp.pad(tb, ((0, 0), (1, 0), (0, 0), (0, 0), (0, 0)))[:, :-1]
        return jnp.concatenate([prev, tb], axis=2)

    kb, vb = band(k), band(v)
    scores = jnp.einsum('bnqhgd,bnkhd->bnhgqk', qb, kb,
                        preferred_element_type=jnp.float32) * (HEAD_DIM ** -0.5)

    qi = np.arange(WINDOW)[:, None]
    kj = np.arange(2 * WINDOW)[None, :]
    dist = qi + WINDOW - kj
    in_window = (dist >= 0) & (dist < WINDOW)
    bucket = t5_bucket(np.clip(dist, 0, None))
    bias = jnp.transpose(rel_bias[bucket].astype(jnp.float32), (2, 0, 1))
    bias = bias.reshape(N_KV_HEADS, GQA_GROUP, WINDOW, 2 * WINDOW)
    block_ok = (np.arange(nb)[:, None] > 0) | (np.arange(2 * WINDOW)[None, :] >= WINDOW)
    mask = in_window[None, None, None] & block_ok[:, None, None, None, :]
    scores = jnp.where(mask, scores + bias, NEG_INF)

    sink = sinks.astype(jnp.float32).reshape(N_KV_HEADS, GQA_GROUP, 1, 1)
    m = jnp.maximum(jnp.max(scores, axis=-1, keepdims=True), sink)
    p = jnp.exp(scores - m)
    denom = jnp.sum(p, axis=-1, keepdims=True) + jnp.exp(sink - m)
    probs = (p / denom).astype(vb.dtype)
    out = jnp.einsum('bnhgqk,bnkhd->bnqhgd', probs, vb)
    return out.reshape(B, S, ATTN_WIDTH)


def spatial_gating(u, v, ln_g, ln_b, w_s, b_s):
    B, S = u.shape[0], u.shape[1]
    nc = S // CHUNK
    vn = layer_norm(v, ln_g, ln_b)
    vc = vn.reshape(B, nc, CHUNK, SGU_GROUPS, SGU_GROUP_DIM)
    causal = np.tril(np.ones((CHUNK, CHUNK), dtype=bool))
    w = jnp.where(causal[None], w_s, jnp.zeros_like(w_s))
    mixed = jnp.einsum('gts,bnsgd->bntgd', w, vc) + b_s.T[:, :, None]
    return u * mixed.reshape(B, S, SGU_WIDTH)


def setup_inputs(seed: int = 0) -> dict:
    key = jax.random.key(seed)
    ks = jax.random.split(key, 16)
    f32 = jnp.float32
    nrm = lambda k, shape, s: jax.random.normal(k, shape, f32) * s
    return {
        "x": nrm(ks[0], (BATCH, SEQ, D_MODEL), 1.0),
        "rel_bias": nrm(ks[1], (N_BUCKETS, N_Q_HEADS), 0.5),
        "pre_norm": 1.0 + nrm(ks[2], (DEPTH, D_MODEL), 0.05),
        "post_norm": 1.0 + nrm(ks[3], (DEPTH, D_MODEL), 0.05),
        "w_in": nrm(ks[4], (DEPTH, D_MODEL, IN_COLS), D_MODEL ** -0.5),
        "sinks": nrm(ks[5], (DEPTH, N_Q_HEADS), 0.5),
        "sgu_ln_g": 1.0 + nrm(ks[6], (DEPTH, SGU_WIDTH), 0.05),
        "sgu_ln_b": nrm(ks[7], (DEPTH, SGU_WIDTH), 0.02),
        "w_spatial": nrm(ks[8], (DEPTH, SGU_GROUPS, CHUNK, CHUNK), CHUNK ** -0.5),
        "b_spatial": 1.0 + nrm(ks[9], (DEPTH, SGU_GROUPS, CHUNK), 0.1),
        "w_branch_a": nrm(ks[10], (DEPTH, ATTN_WIDTH, D_MODEL), ATTN_WIDTH ** -0.5),
        "w_branch_b": nrm(ks[11], (DEPTH, SGU_WIDTH, D_MODEL), SGU_WIDTH ** -0.5),
        "w_out": nrm(ks[12], (DEPTH, D_MODEL, D_MODEL), D_MODEL ** -0.5),
    }


def reference(x, rel_bias, pre_norm, post_norm, w_in, sinks, sgu_ln_g, sgu_ln_b,
              w_spatial, b_spatial, w_branch_a, w_branch_b, w_out):
    B, S = x.shape[0], x.shape[1]
    offsets = [int(o) for o in np.cumsum(SPLITS)[:-1]]
    for l in range(DEPTH):
        h = rms_norm(x, pre_norm[l])
        proj = h @ w_in[l]
        q, k, v, gate_a, u_b, v_b, gate_b, mg_a, mg_b = jnp.split(proj, offsets, axis=-1)
        o_a = sliding_window_attention(
            q.reshape(B, S, N_Q_HEADS, HEAD_DIM),
            k.reshape(B, S, N_KV_HEADS, HEAD_DIM),
            v.reshape(B, S, N_KV_HEADS, HEAD_DIM),
            sinks[l], rel_bias)
        y_a = (o_a * jax.nn.silu(gate_a)) @ w_branch_a[l]
        o_b = spatial_gating(jax.nn.gelu(u_b, approximate=False), jax.nn.gelu(v_b, approximate=False),
                             sgu_ln_g[l], sgu_ln_b[l], w_spatial[l], b_spatial[l])
        y_b = (o_b * jax.nn.silu(gate_b)) @ w_branch_b[l]
        merged = jax.nn.sigmoid(mg_a) * y_a + jax.nn.sigmoid(mg_b) * y_b
        out = merged @ w_out[l]
        x = x + rms_norm(out, post_norm[l])
    return x
```

```python
import functools

import jax
import jax.numpy as jnp
import numpy as np
from jax import lax
from jax.experimental import pallas as pl
from jax.experimental.pallas import tpu as pltpu

D_MODEL = 1024
HEAD_DIM = 64
N_Q_HEADS = 16
N_KV_HEADS = 2
GQA_GROUP = N_Q_HEADS // N_KV_HEADS
WINDOW = 128
ATTN_WIDTH = N_Q_HEADS * HEAD_DIM
KV_WIDTH = N_KV_HEADS * HEAD_DIM
N_BUCKETS = 32
MAX_DISTANCE = 128
SGU_WIDTH = D_MODEL
SGU_GROUPS = 8
SGU_GROUP_DIM = SGU_WIDTH // SGU_GROUPS
CHUNK = 128
EPS = 1e-6
NEG_INF = -1e30

_SPLITS = (ATTN_WIDTH, KV_WIDTH, KV_WIDTH, ATTN_WIDTH, SGU_WIDTH, SGU_WIDTH, SGU_WIDTH, D_MODEL, D_MODEL)
_OFFS = tuple(int(o) for o in np.cumsum((0,) + _SPLITS))
IN_COLS = _OFFS[-1]
(Q_OFF, K_OFF, V_OFF, GA_OFF, U_OFF, VB_OFF, GB_OFF, MA_OFF, MB_OFF) = _OFFS[:-1]

TILE = 256
BLOCKS_PER_TILE = TILE // WINDOW
V7X_VMEM_LIMIT_BYTES = 56 * 1024 * 1024


def _t5_bucket_table():
    qi = np.arange(WINDOW)[:, None]
    kj = np.arange(2 * WINDOW)[None, :]
    dist = np.clip(qi + WINDOW - kj, 0, None)
    max_exact = N_BUCKETS // 2
    large = max_exact + (np.log(np.maximum(dist, max_exact) / max_exact)
                         / np.log(MAX_DISTANCE / max_exact)
                         * (N_BUCKETS - max_exact)).astype(np.int32)
    large = np.minimum(large, N_BUCKETS - 1)
    return np.where(dist < max_exact, dist, large).astype(np.int32)


def _dot(a, b):
    return jnp.dot(a, b, preferred_element_type=jnp.float32)


def _gelu_exact(x):
    return 0.5 * x * (1.0 + lax.erf(x * np.float32(np.sqrt(0.5))))


def _silu(x):
    return x * jax.nn.sigmoid(x)


def _layer_kernel(rel_bias_ref, sinks_ref,
                  x_ref, bucket_ref, pre_ref, post_ref, lng_ref, lnb_ref,
                  w_in_ref, w_sp_ref, b_sp_ref, w_a_ref, w_b_ref, w_o_ref,
                  o_ref,
                  bias_ref, wtril_ref, kx_ref, vx_ref, oa_ref, ob_ref):
    b_idx = pl.program_id(0)
    t_idx = pl.program_id(1)
    bf16 = jnp.bfloat16

    @pl.when(jnp.logical_and(b_idx == 0, t_idx == 0))
    def _():
        bucket = bucket_ref[...]
        for h in range(N_Q_HEADS):
            acc = jnp.zeros((WINDOW, 2 * WINDOW), jnp.float32)
            for bk in range(N_BUCKETS):
                acc = jnp.where(bucket == bk, rel_bias_ref[bk, h], acc)
            bias_ref[h] = acc
        row = lax.broadcasted_iota(jnp.int32, (CHUNK, CHUNK), 0)
        col = lax.broadcasted_iota(jnp.int32, (CHUNK, CHUNK), 1)
        for g in range(SGU_GROUPS):
            wtril_ref[g] = jnp.where(col <= row, w_sp_ref[g], 0.0).astype(bf16)

    @pl.when(t_idx == 0)
    def _():
        kx_ref[0:WINDOW, :] = jnp.zeros((WINDOW, KV_WIDTH), bf16)
        vx_ref[0:WINDOW, :] = jnp.zeros((WINDOW, KV_WIDTH), bf16)

    x = x_ref[0]
    h32 = x * lax.rsqrt(jnp.mean(x * x, axis=-1, keepdims=True) + EPS) * pre_ref[...]
    h = h32.astype(bf16)

    def proj(off, width):
        return _dot(h, w_in_ref[:, off:off + width])

    q = (proj(Q_OFF, ATTN_WIDTH) * np.float32(HEAD_DIM ** -0.5)).astype(bf16)
    kx_ref[WINDOW:, :] = proj(K_OFF, KV_WIDTH).astype(bf16)
    vx_ref[WINDOW:, :] = proj(V_OFF, KV_WIDTH).astype(bf16)

    qi = lax.broadcasted_iota(jnp.int32, (WINDOW, 2 * WINDOW), 0)
    kj = lax.broadcasted_iota(jnp.int32, (WINDOW, 2 * WINDOW), 1)
    dist = qi + WINDOW - kj
    in_window = jnp.logical_and(dist >= 0, dist < WINDOW)
    first_mask = jnp.logical_and(in_window, jnp.logical_or(kj >= WINDOW, t_idx > 0))

    for blk in range(BLOCKS_PER_TILE):
        r0 = blk * WINDOW
        mask = first_mask if blk == 0 else in_window
        head_outs = []
        for j in range(N_KV_HEADS):
            k_band = kx_ref[r0:r0 + 2 * WINDOW, j * HEAD_DIM:(j + 1) * HEAD_DIM]
            v_band = vx_ref[r0:r0 + 2 * WINDOW, j * HEAD_DIM:(j + 1) * HEAD_DIM]
            q_g = jnp.concatenate(
                [q[r0:r0 + WINDOW, (j * GQA_GROUP + i) * HEAD_DIM:(j * GQA_GROUP + i + 1) * HEAD_DIM]
                 for i in range(GQA_GROUP)], axis=0)
            s_g = lax.dot_general(q_g, k_band, (((1,), (1,)), ((), ())),
                                  preferred_element_type=jnp.float32)
            p_parts, inv_parts = [], []
            for i in range(GQA_GROUP):
                hq = j * GQA_GROUP + i
                s = s_g[i * WINDOW:(i + 1) * WINDOW]
                s = jnp.where(mask, s + bias_ref[hq], NEG_INF)
                sink = sinks_ref[hq]
                m = jnp.maximum(jnp.max(s, axis=-1, keepdims=True), sink)
                p = jnp.exp(s - m)
                denom = jnp.sum(p, axis=-1, keepdims=True) + jnp.exp(sink - m)
                p_parts.append(p.astype(bf16))
                inv_parts.append(1.0 / denom)
            p_g = jnp.concatenate(p_parts, axis=0)
            o_g = _dot(p_g, v_band)
            for i in range(GQA_GROUP):
                head_outs.append(o_g[i * WINDOW:(i + 1) * WINDOW] * inv_parts[i])
        oa_ref[r0:r0 + WINDOW, :] = jnp.concatenate(head_outs, axis=1)

    kx_ref[0:WINDOW, :] = kx_ref[TILE:TILE + WINDOW, :]
    vx_ref[0:WINDOW, :] = vx_ref[TILE:TILE + WINDOW, :]

    gate_a = proj(GA_OFF, ATTN_WIDTH)
    y_a = _dot((oa_ref[...] * _silu(gate_a)).astype(bf16), w_a_ref[...])

    v_b = _gelu_exact(proj(VB_OFF, SGU_WIDTH))
    mu = jnp.mean(v_b, axis=-1, keepdims=True)
    vc = v_b - mu
    vn = (vc * lax.rsqrt(jnp.mean(vc * vc, axis=-1, keepdims=True) + EPS) * lng_ref[...]
          + lnb_ref[...]).astype(bf16)
    for c in range(TILE // CHUNK):
        c0 = c * CHUNK
        mixed = []
        for g in range(SGU_GROUPS):
            m_g = _dot(wtril_ref[g], vn[c0:c0 + CHUNK, g * SGU_GROUP_DIM:(g + 1) * SGU_GROUP_DIM])
            mixed.append(m_g + b_sp_ref[g])
        ob_ref[c0:c0 + CHUNK, :] = jnp.concatenate(mixed, axis=1)
    u_b = _gelu_exact(proj(U_OFF, SGU_WIDTH))
    gate_b = proj(GB_OFF, SGU_WIDTH)
    y_b = _dot((u_b * ob_ref[...] * _silu(gate_b)).astype(bf16), w_b_ref[...])

    merged = jax.nn.sigmoid(proj(MA_OFF, D_MODEL)) * y_a + jax.nn.sigmoid(proj(MB_OFF, D_MODEL)) * y_b
    out = _dot(merged.astype(bf16), w_o_ref[...])
    o_ref[0] = x + out * lax.rsqrt(jnp.mean(out * out, axis=-1, keepdims=True) + EPS) * post_ref[...]


def _resident(shape):
    return pl.BlockSpec(shape, lambda b, t, *_: (0,) * len(shape), pipeline_mode=pl.Buffered(1))


def _layer(x, rel_bias, sinks, bucket, pre, post, lng, lnb, w_in, w_sp, b_sp, w_a, w_b, w_o):
    B, S, D = x.shape
    assert D == D_MODEL and S % TILE == 0 and w_in.shape == (D_MODEL, IN_COLS)
    x_spec = pl.BlockSpec((1, TILE, D), lambda b, t, *_: (b, t, 0))
    grid_spec = pltpu.PrefetchScalarGridSpec(
        num_scalar_prefetch=2,
        grid=(B, S // TILE),
        in_specs=[
            x_spec,
            _resident((WINDOW, 2 * WINDOW)),
            _resident((1, D)), _resident((1, D)),
            _resident((1, SGU_WIDTH)), _resident((1, SGU_WIDTH)),
            _resident((D, IN_COLS)),
            _resident((SGU_GROUPS, CHUNK, CHUNK)),
            _resident((SGU_GROUPS, CHUNK, 1)),
            _resident((ATTN_WIDTH, D)), _resident((SGU_WIDTH, D)), _resident((D, D)),
        ],
        out_specs=x_spec,
        scratch_shapes=[
            pltpu.VMEM((N_Q_HEADS, WINDOW, 2 * WINDOW), jnp.float32),
            pltpu.VMEM((SGU_GROUPS, CHUNK, CHUNK), jnp.bfloat16),
            pltpu.VMEM((TILE + WINDOW, KV_WIDTH), jnp.bfloat16),
            pltpu.VMEM((TILE + WINDOW, KV_WIDTH), jnp.bfloat16),
            pltpu.VMEM((TILE, ATTN_WIDTH), jnp.float32),
            pltpu.VMEM((TILE, SGU_WIDTH), jnp.float32),
        ],
    )
    return pl.pallas_call(
        _layer_kernel,
        out_shape=jax.ShapeDtypeStruct(x.shape, x.dtype),
        grid_spec=grid_spec,
        compiler_params=pltpu.CompilerParams(
            dimension_semantics=("arbitrary", "arbitrary"),
            vmem_limit_bytes=V7X_VMEM_LIMIT_BYTES),
        name="hybrid_layer",
    )(rel_bias, sinks, x, bucket, pre, post, lng, lnb, w_in, w_sp, b_sp, w_a, w_b, w_o)


def kernel(x, rel_bias, pre_norm, post_norm, w_in, sinks, sgu_ln_g, sgu_ln_b, w_spatial, b_spatial,
           w_branch_a, w_branch_b, w_out):
    depth = w_in.shape[0]
    bucket = jnp.asarray(_t5_bucket_table())
    bf16 = jnp.bfloat16
    for l in range(depth):
        x = _layer(
            x, rel_bias, sinks[l], bucket,
            pre_norm[l][None], post_norm[l][None], sgu_ln_g[l][None], sgu_ln_b[l][None],
            w_in[l].astype(bf16), w_spatial[l], b_spatial[l][:, :, None],
            w_branch_a[l].astype(bf16), w_branch_b[l].astype(bf16), w_out[l].astype(bf16))
    return x
```

```python
import jax
import jax.numpy as jnp
import numpy as np
from jax import lax
from jax.experimental import pallas as pl
from jax.experimental.pallas import tpu as pltpu

D_MODEL = 1024
HEAD_DIM = 64
N_Q_HEADS = 16
N_KV_HEADS = 2
GQA_GROUP = N_Q_HEADS // N_KV_HEADS
WINDOW = 128
ATTN_WIDTH = N_Q_HEADS * HEAD_DIM
KV_WIDTH = N_KV_HEADS * HEAD_DIM
PAIR_WIDTH = 2 * HEAD_DIM
PAIRS_PER_KV = GQA_GROUP // 2
N_BUCKETS = 32
MAX_DISTANCE = 128
SGU_WIDTH = D_MODEL
SGU_GROUPS = 8
SGU_GROUP_DIM = SGU_WIDTH // SGU_GROUPS
CHUNK = 128
EPS = 1e-6
NEG_INF = -1e30

_SPLITS = (ATTN_WIDTH, KV_WIDTH, KV_WIDTH, ATTN_WIDTH, SGU_WIDTH, SGU_WIDTH, SGU_WIDTH, D_MODEL, D_MODEL)
_OFFS = tuple(int(o) for o in np.cumsum((0,) + _SPLITS))
IN_COLS = _OFFS[-1]
(Q_OFF, K_OFF, V_OFF, GA_OFF, U_OFF, VB_OFF, GB_OFF, MA_OFF, MB_OFF) = _OFFS[:-1]

TILE = 256
BLOCKS_PER_TILE = TILE // WINDOW
V7X_VMEM_LIMIT_BYTES = 56 * 1024 * 1024


def _t5_bucket_table():
    qi = np.arange(WINDOW)[:, None]
    kj = np.arange(2 * WINDOW)[None, :]
    dist = np.clip(qi + WINDOW - kj, 0, None)
    max_exact = N_BUCKETS // 2
    large = max_exact + (np.log(np.maximum(dist, max_exact) / max_exact)
                         / np.log(MAX_DISTANCE / max_exact)
                         * (N_BUCKETS - max_exact)).astype(np.int32)
    large = np.minimum(large, N_BUCKETS - 1)
    return np.where(dist < max_exact, dist, large).astype(np.int32)


def _dot(a, b):
    return jnp.dot(a, b, preferred_element_type=jnp.float32)


def _gelu_exact(x):
    return 0.5 * x * (1.0 + lax.erf(x * np.float32(np.sqrt(0.5))))


def _silu(x):
    return x * jax.nn.sigmoid(x)


def _layer_kernel(rel_bias_ref, sinks_ref,
                  x_ref, bucket_ref, pre_ref, post_ref, lng_ref, lnb_ref,
                  w_in_ref, w_sp_ref, b_sp_ref, w_a_ref, w_b_ref, w_o_ref,
                  o_ref,
                  bias_ref, wtril_ref, kvar_ref, vvar_ref, oa_ref, ob_ref):
    b_idx = pl.program_id(0)
    t_idx = pl.program_id(1)
    bf16 = jnp.bfloat16

    @pl.when(jnp.logical_and(b_idx == 0, t_idx == 0))
    def _():
        bucket = bucket_ref[...]
        for h in range(N_Q_HEADS):
            acc = jnp.zeros((WINDOW, 2 * WINDOW), jnp.float32)
            for bk in range(N_BUCKETS):
                acc = jnp.where(bucket == bk, rel_bias_ref[bk, h], acc)
            bias_ref[h] = acc
        row = lax.broadcasted_iota(jnp.int32, (CHUNK, CHUNK), 0)
        col = lax.broadcasted_iota(jnp.int32, (CHUNK, CHUNK), 1)
        for g in range(SGU_GROUPS):
            wtril_ref[g] = jnp.where(col <= row, w_sp_ref[g], 0.0).astype(bf16)

    @pl.when(t_idx == 0)
    def _():
        for i in range(2 * N_KV_HEADS):
            kvar_ref[i, 0:WINDOW, :] = jnp.zeros((WINDOW, PAIR_WIDTH), bf16)
            vvar_ref[i, 0:WINDOW, :] = jnp.zeros((WINDOW, PAIR_WIDTH), bf16)

    x = x_ref[0]
    h32 = x * lax.rsqrt(jnp.mean(x * x, axis=-1, keepdims=True) + EPS) * pre_ref[...]
    h = h32.astype(bf16)

    def proj(off, width):
        return _dot(h, w_in_ref[:, off:off + width])

    q = (proj(Q_OFF, ATTN_WIDTH) * np.float32(HEAD_DIM ** -0.5)).astype(bf16)
    low = lax.broadcasted_iota(jnp.int32, (TILE, PAIR_WIDTH), 1) < HEAD_DIM

    def place(var_ref, t):
        zero = jnp.zeros_like(t)
        swapped = pltpu.roll(t, HEAD_DIM, 1)
        var_ref[0, WINDOW:, :] = jnp.where(low, t, zero)
        var_ref[1, WINDOW:, :] = jnp.where(low, zero, swapped)
        var_ref[2, WINDOW:, :] = jnp.where(low, swapped, zero)
        var_ref[3, WINDOW:, :] = jnp.where(low, zero, t)

    place(kvar_ref, proj(K_OFF, KV_WIDTH).astype(bf16))
    place(vvar_ref, proj(V_OFF, KV_WIDTH).astype(bf16))

    qi = lax.broadcasted_iota(jnp.int32, (WINDOW, 2 * WINDOW), 0)
    kj = lax.broadcasted_iota(jnp.int32, (WINDOW, 2 * WINDOW), 1)
    dist = qi + WINDOW - kj
    in_window = jnp.logical_and(dist >= 0, dist < WINDOW)
    first_mask = jnp.logical_and(in_window, jnp.logical_or(kj >= WINDOW, t_idx > 0))
    band_low = lax.broadcasted_iota(jnp.int32, (2 * WINDOW, PAIR_WIDTH), 1) < HEAD_DIM
    ones_even = jnp.where(band_low, 1.0, 0.0).astype(bf16)
    ones_odd = jnp.where(band_low, 0.0, 1.0).astype(bf16)
    out_low = lax.broadcasted_iota(jnp.int32, (WINDOW, PAIR_WIDTH), 1) < HEAD_DIM

    def attn_scores(blk, j):
        r0 = blk * WINDOW
        q_g = jnp.concatenate(
            [q[r0:r0 + WINDOW, (PAIRS_PER_KV * j + p) * PAIR_WIDTH:(PAIRS_PER_KV * j + p + 1) * PAIR_WIDTH]
             for p in range(PAIRS_PER_KV)], axis=0)
        k_g = jnp.concatenate([kvar_ref[2 * j + e, r0:r0 + 2 * WINDOW, :] for e in range(2)], axis=0)
        return lax.dot_general(q_g, k_g, (((1,), (1,)), ((), ())),
                               preferred_element_type=jnp.float32)

    def attn_softmax(blk, j, s_g):
        mask = first_mask if blk == 0 else in_window
        rows, sink_terms = [], []
        for p in range(PAIRS_PER_KV):
            probs, terms = [], []
            for e in range(2):
                hq = GQA_GROUP * j + 2 * p + e
                s = s_g[p * WINDOW:(p + 1) * WINDOW, e * 2 * WINDOW:(e + 1) * 2 * WINDOW]
                s = jnp.where(mask, s + bias_ref[hq], NEG_INF)
                sink = sinks_ref[hq]
                m = jnp.maximum(jnp.max(s, axis=-1, keepdims=True), sink)
                probs.append(jnp.exp(s - m).astype(bf16))
                terms.append(jnp.exp(sink - m))
            rows.append(jnp.concatenate(probs, axis=1))
            sink_terms.append(jnp.where(out_low, terms[0], terms[1]))
        return jnp.concatenate(rows, axis=0), sink_terms

    def attn_pv(blk, j, p_g, sink_terms):
        r0 = blk * WINDOW
        v_g = jnp.concatenate(
            [jnp.concatenate([vvar_ref[2 * j + e, r0:r0 + 2 * WINDOW, :], ones], axis=1)
             for e, ones in ((0, ones_even), (1, ones_odd))], axis=0)
        r_g = _dot(p_g, v_g)
        for p in range(PAIRS_PER_KV):
            r = r_g[p * WINDOW:(p + 1) * WINDOW]
            c0 = (PAIRS_PER_KV * j + p) * PAIR_WIDTH
            oa_ref[r0:r0 + WINDOW, c0:c0 + PAIR_WIDTH] = r[:, :PAIR_WIDTH] / (r[:, PAIR_WIDTH:] + sink_terms[p])

    groups = [(blk, j) for blk in range(BLOCKS_PER_TILE) for j in range(N_KV_HEADS)]
    dense = {}
    dense_order = [("v_b", VB_OFF), ("u_b", U_OFF), ("gate_b", GB_OFF), ("gate_a", GA_OFF)]
    s_next = attn_scores(*groups[0])
    for gi, (blk, j) in enumerate(groups):
        s_g = s_next
        if gi < len(dense_order):
            name, off = dense_order[gi]
            dense[name] = proj(off, D_MODEL)
        p_g, sink_terms = attn_softmax(blk, j, s_g)
        if gi + 1 < len(groups):
            s_next = attn_scores(*groups[gi + 1])
        attn_pv(blk, j, p_g, sink_terms)
    for name, off in dense_order[len(groups):]:
        dense[name] = proj(off, D_MODEL)

    for i in range(2 * N_KV_HEADS):
        kvar_ref[i, 0:WINDOW, :] = kvar_ref[i, TILE:TILE + WINDOW, :]
        vvar_ref[i, 0:WINDOW, :] = vvar_ref[i, TILE:TILE + WINDOW, :]

    v_b = _gelu_exact(dense["v_b"])
    mu = jnp.mean(v_b, axis=-1, keepdims=True)
    vc = v_b - mu
    vn = (vc * lax.rsqrt(jnp.mean(vc * vc, axis=-1, keepdims=True) + EPS) * lng_ref[...]
          + lnb_ref[...]).astype(bf16)
    n_chunks = TILE // CHUNK
    for g in range(SGU_GROUPS):
        c_lo = g * SGU_GROUP_DIM
        v_g = jnp.concatenate([vn[c * CHUNK:(c + 1) * CHUNK, c_lo:c_lo + SGU_GROUP_DIM] for c in range(n_chunks)],
                              axis=1)
        m_g = _dot(wtril_ref[g], v_g) + b_sp_ref[g]
        for c in range(n_chunks):
            ob_ref[c * CHUNK:(c + 1) * CHUNK, c_lo:c_lo + SGU_GROUP_DIM] = (
                m_g[:, c * SGU_GROUP_DIM:(c + 1) * SGU_GROUP_DIM])
    mg_a = proj(MA_OFF, D_MODEL)
    y_b = _dot((_gelu_exact(dense["u_b"]) * ob_ref[...] * _silu(dense["gate_b"])).astype(bf16), w_b_ref[...])
    mg_b = proj(MB_OFF, D_MODEL)
    y_a = _dot((oa_ref[...] * _silu(dense["gate_a"])).astype(bf16), w_a_ref[...])

    merged = jax.nn.sigmoid(mg_a) * y_a + jax.nn.sigmoid(mg_b) * y_b
    out = _dot(merged.astype(bf16), w_o_ref[...])
    o_ref[0] = x + out * lax.rsqrt(jnp.mean(out * out, axis=-1, keepdims=True) + EPS) * post_ref[...]


def _resident(shape):
    return pl.BlockSpec(shape, lambda b, t, *_: (0,) * len(shape), pipeline_mode=pl.Buffered(1))


def _layer(x, rel_bias, sinks, bucket, pre, post, lng, lnb, w_in, w_sp, b_sp, w_a, w_b, w_o):
    B, S, D = x.shape
    assert D == D_MODEL and S % TILE == 0 and w_in.shape == (D_MODEL, IN_COLS)
    x_spec = pl.BlockSpec((1, TILE, D), lambda b, t, *_: (b, t, 0))
    grid_spec = pltpu.PrefetchScalarGridSpec(
        num_scalar_prefetch=2,
        grid=(B, S // TILE),
        in_specs=[
            x_spec,
            _resident((WINDOW, 2 * WINDOW)),
            _resident((1, D)), _resident((1, D)),
            _resident((1, SGU_WIDTH)), _resident((1, SGU_WIDTH)),
            _resident((D, IN_COLS)),
            _resident((SGU_GROUPS, CHUNK, CHUNK)),
            _resident((SGU_GROUPS, CHUNK, 1)),
            _resident((ATTN_WIDTH, D)), _resident((SGU_WIDTH, D)), _resident((D, D)),
        ],
        out_specs=x_spec,
        scratch_shapes=[
            pltpu.VMEM((N_Q_HEADS, WINDOW, 2 * WINDOW), jnp.float32),
            pltpu.VMEM((SGU_GROUPS, CHUNK, CHUNK), jnp.bfloat16),
            pltpu.VMEM((2 * N_KV_HEADS, TILE + WINDOW, PAIR_WIDTH), jnp.bfloat16),
            pltpu.VMEM((2 * N_KV_HEADS, TILE + WINDOW, PAIR_WIDTH), jnp.bfloat16),
            pltpu.VMEM((TILE, ATTN_WIDTH), jnp.float32),
            pltpu.VMEM((TILE, SGU_WIDTH), jnp.float32),
        ],
    )
    return pl.pallas_call(
        _layer_kernel,
        out_shape=jax.ShapeDtypeStruct(x.shape, x.dtype),
        grid_spec=grid_spec,
        compiler_params=pltpu.CompilerParams(
            dimension_semantics=("arbitrary", "arbitrary"),
            vmem_limit_bytes=V7X_VMEM_LIMIT_BYTES),
        name="hybrid_layer",
    )(rel_bias, sinks, x, bucket, pre, post, lng, lnb, w_in, w_sp, b_sp, w_a, w_b, w_o)


def kernel(x, rel_bias, pre_norm, post_norm, w_in, sinks, sgu_ln_g, sgu_ln_b, w_spatial, b_spatial,
           w_branch_a, w_branch_b, w_out):
    depth = w_in.shape[0]
    bucket = jnp.asarray(_t5_bucket_table())
    bf16 = jnp.bfloat16
    for l in range(depth):
        x = _layer(
            x, rel_bias, sinks[l], bucket,
            pre_norm[l][None], post_norm[l][None], sgu_ln_g[l][None], sgu_ln_b[l][None],
            w_in[l].astype(bf16), w_spatial[l], b_spatial[l][:, :, None],
            w_branch_a[l].astype(bf16), w_branch_b[l].astype(bf16), w_out[l].astype(bf16))
    return x
```

```python
import functools

import jax
import jax.numpy as jnp
import numpy as np
from jax import lax
from jax.experimental import pallas as pl
from jax.experimental.pallas import tpu as pltpu

D_MODEL = 1024
HEAD_DIM = 64
N_Q_HEADS = 16
N_KV_HEADS = 2
GQA_GROUP = N_Q_HEADS // N_KV_HEADS
WINDOW = 128
ATTN_WIDTH = N_Q_HEADS * HEAD_DIM
KV_WIDTH = N_KV_HEADS * HEAD_DIM
PAIR_WIDTH = 2 * HEAD_DIM
PAIRS_PER_KV = GQA_GROUP // 2
N_BUCKETS = 32
MAX_DISTANCE = 128
SGU_WIDTH = D_MODEL
SGU_GROUPS = 8
SGU_GROUP_DIM = SGU_WIDTH // SGU_GROUPS
CHUNK = 128
EPS = 1e-6
NEG_INF = -1e30

_SPLITS = (ATTN_WIDTH, KV_WIDTH, KV_WIDTH, ATTN_WIDTH, SGU_WIDTH, SGU_WIDTH, SGU_WIDTH, D_MODEL, D_MODEL)
_OFFS = tuple(int(o) for o in np.cumsum((0,) + _SPLITS))
IN_COLS = _OFFS[-1]
(Q_OFF, K_OFF, V_OFF, GA_OFF, U_OFF, VB_OFF, GB_OFF, MA_OFF, MB_OFF) = _OFFS[:-1]

TILE = 256
BLOCKS_PER_TILE = TILE // WINDOW
V7X_VMEM_LIMIT_BYTES = 56 * 1024 * 1024


def _t5_bucket_table():
    qi = np.arange(WINDOW)[:, None]
    kj = np.arange(2 * WINDOW)[None, :]
    dist = np.clip(qi + WINDOW - kj, 0, None)
    max_exact = N_BUCKETS // 2
    large = max_exact + (np.log(np.maximum(dist, max_exact) / max_exact)
                         / np.log(MAX_DISTANCE / max_exact)
                         * (N_BUCKETS - max_exact)).astype(np.int32)
    large = np.minimum(large, N_BUCKETS - 1)
    return np.where(dist < max_exact, dist, large).astype(np.int32)


def _dot(a, b):
    return jnp.dot(a, b, preferred_element_type=jnp.float32)


def _gelu_exact(x):
    return 0.5 * x * (1.0 + lax.erf(x * np.float32(np.sqrt(0.5))))


def _silu(x):
    return x * jax.nn.sigmoid(x)


def _layer_kernel(rel_bias_ref, sinks_ref,
                  x_ref, bucket_ref, pre_ref, post_ref, lng_ref, lnb_ref,
                  w_in_ref, w_sp_ref, b_sp_ref, w_a_ref, w_b_ref, w_o_ref,
                  o_ref,
                  bias_ref, wtril_ref, kvar_ref, vvar_ref, oa_ref, ob_ref, merged_ref, xkeep_ref,
                  *, tiles_per_seq):
    step = pl.program_id(0)
    t_idx = jnp.minimum(step, pl.num_programs(0) - 2) % tiles_per_seq
    bf16 = jnp.bfloat16

    @pl.when(step == 0)
    def _():
        merged_ref[...] = jnp.zeros_like(merged_ref)
        xkeep_ref[...] = jnp.zeros_like(xkeep_ref)
        bucket = bucket_ref[...]
        for h in range(N_Q_HEADS):
            acc = jnp.zeros((WINDOW, 2 * WINDOW), jnp.float32)
            for bk in range(N_BUCKETS):
                acc = jnp.where(bucket == bk, rel_bias_ref[bk, h], acc)
            bias_ref[h] = acc
        row = lax.broadcasted_iota(jnp.int32, (CHUNK, CHUNK), 0)
        col = lax.broadcasted_iota(jnp.int32, (CHUNK, CHUNK), 1)
        for g in range(SGU_GROUPS):
            wtril_ref[g] = jnp.where(col <= row, w_sp_ref[g], 0.0).astype(bf16)

    @pl.when(t_idx == 0)
    def _():
        for i in range(2 * N_KV_HEADS):
            kvar_ref[i, 0:WINDOW, :] = jnp.zeros((WINDOW, PAIR_WIDTH), bf16)
            vvar_ref[i, 0:WINDOW, :] = jnp.zeros((WINDOW, PAIR_WIDTH), bf16)

    out = _dot(merged_ref[...], w_o_ref[...])
    o_ref[0] = (xkeep_ref[...]
                + out * lax.rsqrt(jnp.mean(out * out, axis=-1, keepdims=True) + EPS) * post_ref[...])

    x = x_ref[0]
    h32 = x * lax.rsqrt(jnp.mean(x * x, axis=-1, keepdims=True) + EPS) * pre_ref[...]
    h = h32.astype(bf16)
    xkeep_ref[...] = x

    def proj(off, width):
        return _dot(h, w_in_ref[:, off:off + width])

    q = (proj(Q_OFF, ATTN_WIDTH) * np.float32(HEAD_DIM ** -0.5)).astype(bf16)
    kv = proj(K_OFF, 2 * KV_WIDTH)
    low = lax.broadcasted_iota(jnp.int32, (TILE, PAIR_WIDTH), 1) < HEAD_DIM

    def place(var_ref, t):
        zero = jnp.zeros_like(t)
        swapped = pltpu.roll(t, HEAD_DIM, 1)
        var_ref[0, WINDOW:, :] = jnp.where(low, t, zero)
        var_ref[1, WINDOW:, :] = jnp.where(low, zero, swapped)
        var_ref[2, WINDOW:, :] = jnp.where(low, swapped, zero)
        var_ref[3, WINDOW:, :] = jnp.where(low, zero, t)

    place(kvar_ref, kv[:, :KV_WIDTH].astype(bf16))
    place(vvar_ref, kv[:, KV_WIDTH:].astype(bf16))

    qi = lax.broadcasted_iota(jnp.int32, (WINDOW, 2 * WINDOW), 0)
    kj = lax.broadcasted_iota(jnp.int32, (WINDOW, 2 * WINDOW), 1)
    dist = qi + WINDOW - kj
    in_window = jnp.logical_and(dist >= 0, dist < WINDOW)
    first_mask = jnp.logical_and(in_window, jnp.logical_or(kj >= WINDOW, t_idx > 0))
    band_low = lax.broadcasted_iota(jnp.int32, (2 * WINDOW, PAIR_WIDTH), 1) < HEAD_DIM
    ones_even = jnp.where(band_low, 1.0, 0.0).astype(bf16)
    ones_odd = jnp.where(band_low, 0.0, 1.0).astype(bf16)
    out_low = lax.broadcasted_iota(jnp.int32, (WINDOW, PAIR_WIDTH), 1) < HEAD_DIM

    def attn_scores(blk, j):
        r0 = blk * WINDOW
        q_g = jnp.concatenate(
            [q[r0:r0 + WINDOW, (PAIRS_PER_KV * j + p) * PAIR_WIDTH:(PAIRS_PER_KV * j + p + 1) * PAIR_WIDTH]
             for p in range(PAIRS_PER_KV)], axis=0)
        k_g = jnp.concatenate([kvar_ref[2 * j + e, r0:r0 + 2 * WINDOW, :] for e in range(2)], axis=0)
        return lax.dot_general(q_g, k_g, (((1,), (1,)), ((), ())),
                               preferred_element_type=jnp.float32)

    def attn_softmax(blk, j, s_g):
        mask = first_mask if blk == 0 else in_window
        rows, sink_terms = [], []
        for p in range(PAIRS_PER_KV):
            probs, terms = [], []
            for e in range(2):
                hq = GQA_GROUP * j + 2 * p + e
                s = s_g[p * WINDOW:(p + 1) * WINDOW, e * 2 * WINDOW:(e + 1) * 2 * WINDOW]
                s = jnp.where(mask, s + bias_ref[hq], NEG_INF)
                sink = sinks_ref[hq]
                m = jnp.maximum(jnp.max(s, axis=-1, keepdims=True), sink)
                probs.append(jnp.exp(s - m).astype(bf16))
                terms.append(jnp.exp(sink - m))
            rows.append(jnp.concatenate(probs, axis=1))
            sink_terms.append(jnp.where(out_low, terms[0], terms[1]))
        return jnp.concatenate(rows, axis=0), sink_terms

    def attn_pv(blk, j, p_g, sink_terms):
        r0 = blk * WINDOW
        v_g = jnp.concatenate(
            [jnp.concatenate([vvar_ref[2 * j + e, r0:r0 + 2 * WINDOW, :], ones], axis=1)
             for e, ones in ((0, ones_even), (1, ones_odd))], axis=0)
        r_g = _dot(p_g, v_g)
        for p in range(PAIRS_PER_KV):
            r = r_g[p * WINDOW:(p + 1) * WINDOW]
            c0 = (PAIRS_PER_KV * j + p) * PAIR_WIDTH
            oa_ref[r0:r0 + WINDOW, c0:c0 + PAIR_WIDTH] = r[:, :PAIR_WIDTH] / (r[:, PAIR_WIDTH:] + sink_terms[p])

    def spatial_mix(v_lin):
        v_act = _gelu_exact(v_lin)
        mu = jnp.mean(v_act, axis=-1, keepdims=True)
        vc = v_act - mu
        vn = (vc * lax.rsqrt(jnp.mean(vc * vc, axis=-1, keepdims=True) + EPS) * lng_ref[...]
              + lnb_ref[...]).astype(bf16)
        n_chunks = TILE // CHUNK
        for g in range(SGU_GROUPS):
            c_lo = g * SGU_GROUP_DIM
            v_g = jnp.concatenate(
                [vn[c * CHUNK:(c + 1) * CHUNK, c_lo:c_lo + SGU_GROUP_DIM] for c in range(n_chunks)], axis=1)
            m_g = _dot(wtril_ref[g], v_g) + b_sp_ref[g]
            for c in range(n_chunks):
                ob_ref[c * CHUNK:(c + 1) * CHUNK, c_lo:c_lo + SGU_GROUP_DIM] = (
                    m_g[:, c * SGU_GROUP_DIM:(c + 1) * SGU_GROUP_DIM])

    groups = [(blk, j) for blk in range(BLOCKS_PER_TILE) for j in range(N_KV_HEADS)]
    assert len(groups) == 4
    s_g = attn_scores(*groups[0])
    v_lin = proj(VB_OFF, SGU_WIDTH)
    p_g, sink_terms = attn_softmax(*groups[0], s_g)
    s_g = attn_scores(*groups[1])
    attn_pv(*groups[0], p_g, sink_terms)
    u_lin = proj(U_OFF, SGU_WIDTH)
    p_g, sink_terms = attn_softmax(*groups[1], s_g)
    s_g = attn_scores(*groups[2])
    attn_pv(*groups[1], p_g, sink_terms)
    spatial_mix(v_lin)
    gate_b = proj(GB_OFF, SGU_WIDTH)
    p_g, sink_terms = attn_softmax(*groups[2], s_g)
    s_g = attn_scores(*groups[3])
    attn_pv(*groups[2], p_g, sink_terms)
    gate_a = proj(GA_OFF, ATTN_WIDTH)
    p_g, sink_terms = attn_softmax(*groups[3], s_g)
    attn_pv(*groups[3], p_g, sink_terms)

    for i in range(2 * N_KV_HEADS):
        kvar_ref[i, 0:WINDOW, :] = kvar_ref[i, TILE:TILE + WINDOW, :]
        vvar_ref[i, 0:WINDOW, :] = vvar_ref[i, TILE:TILE + WINDOW, :]

    mg_b = proj(MB_OFF, D_MODEL)
    mg_a = proj(MA_OFF, D_MODEL)
    y_b = _dot((_gelu_exact(u_lin) * ob_ref[...] * _silu(gate_b)).astype(bf16), w_b_ref[...])
    y_a = _dot((oa_ref[...] * _silu(gate_a)).astype(bf16), w_a_ref[...])

    merged_ref[...] = (jax.nn.sigmoid(mg_a) * y_a + jax.nn.sigmoid(mg_b) * y_b).astype(bf16)


def _resident(shape):
    return pl.BlockSpec(shape, lambda s, *_: (0,) * len(shape), pipeline_mode=pl.Buffered(1))


def _layer(x, rel_bias, sinks, bucket, pre, post, lng, lnb, w_in, w_sp, b_sp, w_a, w_b, w_o):
    B, S, D = x.shape
    assert D == D_MODEL and S % TILE == 0 and w_in.shape == (D_MODEL, IN_COLS)
    tiles_per_seq = S // TILE
    n_tiles = B * tiles_per_seq

    def tile_index(tile):
        return (tile // tiles_per_seq, tile % tiles_per_seq, 0)

    x_spec = pl.BlockSpec((1, TILE, D), lambda s, *_: tile_index(jnp.minimum(s, n_tiles - 1)))
    o_spec = pl.BlockSpec((1, TILE, D), lambda s, *_: tile_index(jnp.maximum(s - 1, 0)))
    grid_spec = pltpu.PrefetchScalarGridSpec(
        num_scalar_prefetch=2,
        grid=(n_tiles + 1,),
        in_specs=[
            x_spec,
            _resident((WINDOW, 2 * WINDOW)),
            _resident((1, D)), _resident((1, D)),
            _resident((1, SGU_WIDTH)), _resident((1, SGU_WIDTH)),
            _resident((D, IN_COLS)),
            _resident((SGU_GROUPS, CHUNK, CHUNK)),
            _resident((SGU_GROUPS, CHUNK, 1)),
            _resident((ATTN_WIDTH, D)), _resident((SGU_WIDTH, D)), _resident((D, D)),
        ],
        out_specs=o_spec,
        scratch_shapes=[
            pltpu.VMEM((N_Q_HEADS, WINDOW, 2 * WINDOW), jnp.float32),
            pltpu.VMEM((SGU_GROUPS, CHUNK, CHUNK), jnp.bfloat16),
            pltpu.VMEM((2 * N_KV_HEADS, TILE + WINDOW, PAIR_WIDTH), jnp.bfloat16),
            pltpu.VMEM((2 * N_KV_HEADS, TILE + WINDOW, PAIR_WIDTH), jnp.bfloat16),
            pltpu.VMEM((TILE, ATTN_WIDTH), jnp.float32),
            pltpu.VMEM((TILE, SGU_WIDTH), jnp.float32),
            pltpu.VMEM((TILE, D), jnp.bfloat16),
            pltpu.VMEM((TILE, D), jnp.float32),
        ],
    )
    return pl.pallas_call(
        functools.partial(_layer_kernel, tiles_per_seq=tiles_per_seq),
        out_shape=jax.ShapeDtypeStruct(x.shape, x.dtype),
        grid_spec=grid_spec,
        compiler_params=pltpu.CompilerParams(
            dimension_semantics=("arbitrary",),
            vmem_limit_bytes=V7X_VMEM_LIMIT_BYTES),
        name="hybrid_layer",
    )(rel_bias, sinks, x, bucket, pre, post, lng, lnb, w_in, w_sp, b_sp, w_a, w_b, w_o)


def kernel(x, rel_bias, pre_norm, post_norm, w_in, sinks, sgu_ln_g, sgu_ln_b, w_spatial, b_spatial,
           w_branch_a, w_branch_b, w_out):
    depth = w_in.shape[0]
    bucket = jnp.asarray(_t5_bucket_table())
    bf16 = jnp.bfloat16
    for l in range(depth):
        x = _layer(
            x, rel_bias, sinks[l], bucket,
            pre_norm[l][None], post_norm[l][None], sgu_ln_g[l][None], sgu_ln_b[l][None],
            w_in[l].astype(bf16), w_spatial[l], b_spatial[l][:, :, None],
            w_branch_a[l].astype(bf16), w_branch_b[l].astype(bf16), w_out[l].astype(bf16))
    return x
```

```python
import functools

import jax
import jax.numpy as jnp
import numpy as np
from jax import lax
from jax.experimental import pallas as pl
from jax.experimental.pallas import tpu as pltpu

D_MODEL = 1024
HEAD_DIM = 64
N_Q_HEADS = 16
N_KV_HEADS = 2
GQA_GROUP = N_Q_HEADS // N_KV_HEADS
WINDOW = 128
ATTN_WIDTH = N_Q_HEADS * HEAD_DIM
KV_WIDTH = N_KV_HEADS * HEAD_DIM
PAIR_WIDTH = 2 * HEAD_DIM
PAIRS_PER_KV = GQA_GROUP // 2
N_BUCKETS = 32
MAX_DISTANCE = 128
SGU_WIDTH = D_MODEL
SGU_GROUPS = 8
SGU_GROUP_DIM = SGU_WIDTH // SGU_GROUPS
CHUNK = 128
EPS = 1e-6
NEG_INF = -1e30

_SPLITS = (ATTN_WIDTH, KV_WIDTH, KV_WIDTH, ATTN_WIDTH, SGU_WIDTH, SGU_WIDTH, SGU_WIDTH, D_MODEL, D_MODEL)
_OFFS = tuple(int(o) for o in np.cumsum((0,) + _SPLITS))
IN_COLS = _OFFS[-1]
(Q_OFF, K_OFF, V_OFF, GA_OFF, U_OFF, VB_OFF, GB_OFF, MA_OFF, MB_OFF) = _OFFS[:-1]

TILE = 256
BLOCKS_PER_TILE = TILE // WINDOW
V7X_VMEM_LIMIT_BYTES = 56 * 1024 * 1024


def _t5_bucket_table():
    qi = np.arange(WINDOW)[:, None]
    kj = np.arange(2 * WINDOW)[None, :]
    dist = np.clip(qi + WINDOW - kj, 0, None)
    max_exact = N_BUCKETS // 2
    large = max_exact + (np.log(np.maximum(dist, max_exact) / max_exact)
                         / np.log(MAX_DISTANCE / max_exact)
                         * (N_BUCKETS - max_exact)).astype(np.int32)
    large = np.minimum(large, N_BUCKETS - 1)
    return np.where(dist < max_exact, dist, large).astype(np.int32)


def _dot(a, b):
    return jnp.dot(a, b, preferred_element_type=jnp.float32)


def _gelu_exact(x):
    return 0.5 * x * (1.0 + lax.erf(x * np.float32(np.sqrt(0.5))))


def _silu(x):
    return x * jax.nn.sigmoid(x)


def _layer_kernel(rel_bias_ref, sinks_ref,
                  x_ref, bucket_ref, pre_ref, post_ref, lng_ref, lnb_ref,
                  w_in_ref, w_sp_ref, b_sp_ref, w_a_ref, w_b_ref, w_o_ref,
                  o_ref,
                  bias_ref, wtril_ref, kvar_ref, vvar_ref, merged_ref, xkeep_ref,
                  h_ref, q_ref, ga_ref, yain_ref, vn_ref, gu_ref, t_ref, ybin_ref, sb_ref, sa_ref, tb_ref,
                  *, tiles_per_seq):
    step = pl.program_id(0)
    t_idx = jnp.minimum(step, pl.num_programs(0) - 2) % tiles_per_seq
    bf16 = jnp.bfloat16

    @pl.when(step == 0)
    def _():
        merged_ref[...] = jnp.zeros_like(merged_ref)
        xkeep_ref[...] = jnp.zeros_like(xkeep_ref)
        bucket = bucket_ref[...]
        for h in range(N_Q_HEADS):
            acc = jnp.zeros((WINDOW, 2 * WINDOW), jnp.float32)
            for bk in range(N_BUCKETS):
                acc = jnp.where(bucket == bk, rel_bias_ref[bk, h], acc)
            bias_ref[h] = acc
        row = lax.broadcasted_iota(jnp.int32, (CHUNK, CHUNK), 0)
        col = lax.broadcasted_iota(jnp.int32, (CHUNK, CHUNK), 1)
        for g in range(SGU_GROUPS):
            wtril_ref[g] = jnp.where(col <= row, w_sp_ref[g], 0.0).astype(bf16)

    @pl.when(t_idx == 0)
    def _():
        for i in range(2 * N_KV_HEADS):
            kvar_ref[i, 0:WINDOW, :] = jnp.zeros((WINDOW, PAIR_WIDTH), bf16)
            vvar_ref[i, 0:WINDOW, :] = jnp.zeros((WINDOW, PAIR_WIDTH), bf16)

    out = _dot(merged_ref[...], w_o_ref[...])
    o_ref[0] = (xkeep_ref[...]
                + out * lax.rsqrt(jnp.mean(out * out, axis=-1, keepdims=True) + EPS) * post_ref[...])

    x = x_ref[0]
    h_ref[...] = (x * lax.rsqrt(jnp.mean(x * x, axis=-1, keepdims=True) + EPS) * pre_ref[...]).astype(bf16)
    xkeep_ref[...] = x

    def proj(off, width):
        return _dot(h_ref[...], w_in_ref[:, off:off + width])

    q_ref[...] = (proj(Q_OFF, ATTN_WIDTH) * np.float32(HEAD_DIM ** -0.5)).astype(bf16)
    w_kv = w_in_ref[:, K_OFF:K_OFF + 2 * KV_WIDTH]
    kv = jnp.concatenate([_dot(h_ref[:TILE // 2, :], w_kv), _dot(h_ref[TILE // 2:, :], w_kv)], axis=0)
    low = lax.broadcasted_iota(jnp.int32, (TILE, PAIR_WIDTH), 1) < HEAD_DIM

    def place(var_ref, t):
        zero = jnp.zeros_like(t)
        swapped = pltpu.roll(t, HEAD_DIM, 1)
        var_ref[0, WINDOW:, :] = jnp.where(low, t, zero)
        var_ref[1, WINDOW:, :] = jnp.where(low, zero, swapped)
        var_ref[2, WINDOW:, :] = jnp.where(low, swapped, zero)
        var_ref[3, WINDOW:, :] = jnp.where(low, zero, t)

    place(kvar_ref, kv[:, :KV_WIDTH].astype(bf16))
    place(vvar_ref, kv[:, KV_WIDTH:].astype(bf16))

    qi = lax.broadcasted_iota(jnp.int32, (WINDOW, 2 * WINDOW), 0)
    kj = lax.broadcasted_iota(jnp.int32, (WINDOW, 2 * WINDOW), 1)
    dist = qi + WINDOW - kj
    in_window = jnp.logical_and(dist >= 0, dist < WINDOW)
    first_mask = jnp.logical_and(in_window, jnp.logical_or(kj >= WINDOW, t_idx > 0))
    band_low = lax.broadcasted_iota(jnp.int32, (2 * WINDOW, PAIR_WIDTH), 1) < HEAD_DIM
    ones_even = jnp.where(band_low, 1.0, 0.0).astype(bf16)
    ones_odd = jnp.where(band_low, 0.0, 1.0).astype(bf16)
    out_low = lax.broadcasted_iota(jnp.int32, (WINDOW, PAIR_WIDTH), 1) < HEAD_DIM

    def attn_scores(blk, j):
        r0 = blk * WINDOW
        q_g = jnp.concatenate(
            [q_ref[r0:r0 + WINDOW, (PAIRS_PER_KV * j + p) * PAIR_WIDTH:(PAIRS_PER_KV * j + p + 1) * PAIR_WIDTH]
             for p in range(PAIRS_PER_KV)], axis=0)
        k_g = jnp.concatenate([kvar_ref[2 * j + e, r0:r0 + 2 * WINDOW, :] for e in range(2)], axis=0)
        return lax.dot_general(q_g, k_g, (((1,), (1,)), ((), ())),
                               preferred_element_type=jnp.float32)

    def attn_softmax(blk, j, s_g):
        mask = first_mask if blk == 0 else in_window
        rows, sink_terms = [], []
        for p in range(PAIRS_PER_KV):
            probs, terms = [], []
            for e in range(2):
                hq = GQA_GROUP * j + 2 * p + e
                s = s_g[p * WINDOW:(p + 1) * WINDOW, e * 2 * WINDOW:(e + 1) * 2 * WINDOW]
                s = jnp.where(mask, s + bias_ref[hq], NEG_INF)
                sink = sinks_ref[hq]
                m = jnp.maximum(jnp.max(s, axis=-1, keepdims=True), sink)
                probs.append(jnp.exp(s - m).astype(bf16))
                terms.append(jnp.exp(sink - m))
            rows.append(jnp.concatenate(probs, axis=1))
            sink_terms.append(jnp.where(out_low, terms[0], terms[1]))
        return jnp.concatenate(rows, axis=0), sink_terms

    def attn_pv(blk, j, p_g, sink_terms):
        r0 = blk * WINDOW
        v_g = jnp.concatenate(
            [jnp.concatenate([vvar_ref[2 * j + e, r0:r0 + 2 * WINDOW, :], ones], axis=1)
             for e, ones in ((0, ones_even), (1, ones_odd))], axis=0)
        r_g = _dot(p_g, v_g)
        for p in range(PAIRS_PER_KV):
            r = r_g[p * WINDOW:(p + 1) * WINDOW]
            c0 = (PAIRS_PER_KV * j + p) * PAIR_WIDTH
            o_pair = r[:, :PAIR_WIDTH] / (r[:, PAIR_WIDTH:] + sink_terms[p])
            yain_ref[r0:r0 + WINDOW, c0:c0 + PAIR_WIDTH] = (
                o_pair * ga_ref[r0:r0 + WINDOW, c0:c0 + PAIR_WIDTH]).astype(bf16)

    def spatial_mix():
        n_chunks = TILE // CHUNK
        for g in range(SGU_GROUPS):
            c_lo = g * SGU_GROUP_DIM
            v_g = jnp.concatenate(
                [vn_ref[c * CHUNK:(c + 1) * CHUNK, c_lo:c_lo + SGU_GROUP_DIM] for c in range(n_chunks)], axis=1)
            m_g = _dot(wtril_ref[g], v_g) + b_sp_ref[g]
            for c in range(n_chunks):
                rows = slice(c * CHUNK, (c + 1) * CHUNK)
                cols = slice(c_lo, c_lo + SGU_GROUP_DIM)
                ybin_ref[rows, cols] = (m_g[:, c * SGU_GROUP_DIM:(c + 1) * SGU_GROUP_DIM]
                                        * t_ref[rows, cols]).astype(bf16)

    groups = [(blk, j) for blk in range(BLOCKS_PER_TILE) for j in range(N_KV_HEADS)]
    assert len(groups) == 4
    ga_ref[...] = _silu(proj(GA_OFF, ATTN_WIDTH)).astype(bf16)

    s_g = attn_scores(*groups[0])
    v_act = _gelu_exact(proj(VB_OFF, SGU_WIDTH))
    mu = jnp.mean(v_act, axis=-1, keepdims=True)
    vc = v_act - mu
    vn_ref[...] = (vc * lax.rsqrt(jnp.mean(vc * vc, axis=-1, keepdims=True) + EPS) * lng_ref[...]
                   + lnb_ref[...]).astype(bf16)
    p_g, sink_terms = attn_softmax(*groups[0], s_g)
    s_g = attn_scores(*groups[1])
    attn_pv(*groups[0], p_g, sink_terms)

    gu_ref[...] = _gelu_exact(proj(U_OFF, SGU_WIDTH)).astype(bf16)
    p_g, sink_terms = attn_softmax(*groups[1], s_g)
    s_g = attn_scores(*groups[2])
    attn_pv(*groups[1], p_g, sink_terms)

    t_ref[...] = (_silu(proj(GB_OFF, SGU_WIDTH)) * gu_ref[...]).astype(bf16)
    p_g, sink_terms = attn_softmax(*groups[2], s_g)
    s_g = attn_scores(*groups[3])
    attn_pv(*groups[2], p_g, sink_terms)

    spatial_mix()
    sb_ref[...] = jax.nn.sigmoid(proj(MB_OFF, D_MODEL)).astype(bf16)
    p_g, sink_terms = attn_softmax(*groups[3], s_g)
    attn_pv(*groups[3], p_g, sink_terms)

    for i in range(2 * N_KV_HEADS):
        kvar_ref[i, 0:WINDOW, :] = kvar_ref[i, TILE:TILE + WINDOW, :]
        vvar_ref[i, 0:WINDOW, :] = vvar_ref[i, TILE:TILE + WINDOW, :]

    tb_ref[...] = _dot(ybin_ref[...], w_b_ref[...]) * sb_ref[...]
    sa_ref[...] = jax.nn.sigmoid(proj(MA_OFF, D_MODEL)).astype(bf16)
    merged_ref[...] = (_dot(yain_ref[...], w_a_ref[...]) * sa_ref[...] + tb_ref[...]).astype(bf16)


def _resident(shape):
    return pl.BlockSpec(shape, lambda s, *_: (0,) * len(shape), pipeline_mode=pl.Buffered(1))


def _layer(x, rel_bias, sinks, bucket, pre, post, lng, lnb, w_in, w_sp, b_sp, w_a, w_b, w_o):
    B, S, D = x.shape
    assert D == D_MODEL and S % TILE == 0 and w_in.shape == (D_MODEL, IN_COLS)
    tiles_per_seq = S // TILE
    n_tiles = B * tiles_per_seq

    def tile_index(tile):
        return (tile // tiles_per_seq, tile % tiles_per_seq, 0)

    x_spec = pl.BlockSpec((1, TILE, D), lambda s, *_: tile_index(jnp.minimum(s, n_tiles - 1)))
    o_spec = pl.BlockSpec((1, TILE, D), lambda s, *_: tile_index(jnp.maximum(s - 1, 0)))
    stage_bf16 = pltpu.VMEM((TILE, D), jnp.bfloat16)
    stage_f32 = pltpu.VMEM((TILE, D), jnp.float32)
    grid_spec = pltpu.PrefetchScalarGridSpec(
        num_scalar_prefetch=2,
        grid=(n_tiles + 1,),
        in_specs=[
            x_spec,
            _resident((WINDOW, 2 * WINDOW)),
            _resident((1, D)), _resident((1, D)),
            _resident((1, SGU_WIDTH)), _resident((1, SGU_WIDTH)),
            _resident((D, IN_COLS)),
            _resident((SGU_GROUPS, CHUNK, CHUNK)),
            _resident((SGU_GROUPS, CHUNK, 1)),
            _resident((ATTN_WIDTH, D)), _resident((SGU_WIDTH, D)), _resident((D, D)),
        ],
        out_specs=o_spec,
        scratch_shapes=[
            pltpu.VMEM((N_Q_HEADS, WINDOW, 2 * WINDOW), jnp.float32),
            pltpu.VMEM((SGU_GROUPS, CHUNK, CHUNK), jnp.bfloat16),
            pltpu.VMEM((2 * N_KV_HEADS, TILE + WINDOW, PAIR_WIDTH), jnp.bfloat16),
            pltpu.VMEM((2 * N_KV_HEADS, TILE + WINDOW, PAIR_WIDTH), jnp.bfloat16),
            stage_bf16,
            stage_f32,
            stage_bf16, stage_bf16,
            stage_bf16, stage_bf16,
            stage_bf16, stage_bf16, stage_bf16,
            stage_bf16, stage_bf16, stage_bf16,
            stage_f32,
        ],
    )
    return pl.pallas_call(
        functools.partial(_layer_kernel, tiles_per_seq=tiles_per_seq),
        out_shape=jax.ShapeDtypeStruct(x.shape, x.dtype),
        grid_spec=grid_spec,
        compiler_params=pltpu.CompilerParams(
            dimension_semantics=("arbitrary",),
            vmem_limit_bytes=V7X_VMEM_LIMIT_BYTES),
        name="hybrid_layer",
    )(rel_bias, sinks, x, bucket, pre, post, lng, lnb, w_in, w_sp, b_sp, w_a, w_b, w_o)


def kernel(x, rel_bias, pre_norm, post_norm, w_in, sinks, sgu_ln_g, sgu_ln_b, w_spatial, b_spatial,
           w_branch_a, w_branch_b, w_out):
    depth = w_in.shape[0]
    bucket = jnp.asarray(_t5_bucket_table())
    bf16 = jnp.bfloat16
    for l in range(depth):
        x = _layer(
            x, rel_bias, sinks[l], bucket,
            pre_norm[l][None], post_norm[l][None], sgu_ln_g[l][None], sgu_ln_b[l][None],
            w_in[l].astype(bf16), w_spatial[l], b_spatial[l][:, :, None],
            w_branch_a[l].astype(bf16), w_branch_b[l].astype(bf16), w_out[l].astype(bf16))
    return x
```

```python
import functools

import jax
import jax.numpy as jnp
import numpy as np
from jax import lax
from jax.experimental import pallas as pl
from jax.experimental.pallas import tpu as pltpu

D_MODEL = 1024
HEAD_DIM = 64
N_Q_HEADS = 16
N_KV_HEADS = 2
GQA_GROUP = N_Q_HEADS // N_KV_HEADS
WINDOW = 128
ATTN_WIDTH = N_Q_HEADS * HEAD_DIM
KV_WIDTH = N_KV_HEADS * HEAD_DIM
PAIR_WIDTH = 2 * HEAD_DIM
PAIRS_PER_KV = GQA_GROUP // 2
N_BUCKETS = 32
MAX_DISTANCE = 128
SGU_WIDTH = D_MODEL
SGU_GROUPS = 8
SGU_GROUP_DIM = SGU_WIDTH // SGU_GROUPS
CHUNK = 128
EPS = 1e-6
NEG_INF = -1e30

_SPLITS = (ATTN_WIDTH, KV_WIDTH, KV_WIDTH, ATTN_WIDTH, SGU_WIDTH, SGU_WIDTH, SGU_WIDTH, D_MODEL, D_MODEL)
_OFFS = tuple(int(o) for o in np.cumsum((0,) + _SPLITS))
IN_COLS = _OFFS[-1]
(Q_OFF, K_OFF, V_OFF, GA_OFF, U_OFF, VB_OFF, GB_OFF, MA_OFF, MB_OFF) = _OFFS[:-1]

TILE = 512
BLOCKS_PER_TILE = TILE // WINDOW
V7X_VMEM_LIMIT_BYTES = 56 * 1024 * 1024


def _t5_bucket_table():
    qi = np.arange(WINDOW)[:, None]
    kj = np.arange(2 * WINDOW)[None, :]
    dist = np.clip(qi + WINDOW - kj, 0, None)
    max_exact = N_BUCKETS // 2
    large = max_exact + (np.log(np.maximum(dist, max_exact) / max_exact)
                         / np.log(MAX_DISTANCE / max_exact)
                         * (N_BUCKETS - max_exact)).astype(np.int32)
    large = np.minimum(large, N_BUCKETS - 1)
    return np.where(dist < max_exact, dist, large).astype(np.int32)


def _dot(a, b):
    return jnp.dot(a, b, preferred_element_type=jnp.float32)


def _gelu_exact(x):
    return 0.5 * x * (1.0 + lax.erf(x * np.float32(np.sqrt(0.5))))


def _silu(x):
    return x * jax.nn.sigmoid(x)


def _layer_kernel(rel_bias_ref, sinks_ref,
                  x_ref, bucket_ref, pre_ref, post_ref, lng_ref, lnb_ref,
                  w_in_ref, w_sp_ref, b_sp_ref, w_a_ref, w_b_ref, w_o_ref,
                  o_ref,
                  bias_ref, wtril_ref, kvar_ref, vvar_ref, merged_ref, xkeep_ref,
                  h_ref, q_ref, ga_ref, yain_ref, vn_ref, gu_ref, t_ref, ybin_ref, sb_ref, sa_ref, tb_ref,
                  *, tiles_per_seq):
    step = pl.program_id(0)
    t_idx = jnp.minimum(step, pl.num_programs(0) - 2) % tiles_per_seq
    bf16 = jnp.bfloat16

    @pl.when(step == 0)
    def _():
        merged_ref[...] = jnp.zeros_like(merged_ref)
        xkeep_ref[...] = jnp.zeros_like(xkeep_ref)
        bucket = bucket_ref[...]
        for h in range(N_Q_HEADS):
            acc = jnp.zeros((WINDOW, 2 * WINDOW), jnp.float32)
            for bk in range(N_BUCKETS):
                acc = jnp.where(bucket == bk, rel_bias_ref[bk, h], acc)
            bias_ref[h] = acc
        row = lax.broadcasted_iota(jnp.int32, (CHUNK, CHUNK), 0)
        col = lax.broadcasted_iota(jnp.int32, (CHUNK, CHUNK), 1)
        for g in range(SGU_GROUPS):
            wtril_ref[g] = jnp.where(col <= row, w_sp_ref[g], 0.0).astype(bf16)

    @pl.when(t_idx == 0)
    def _():
        for i in range(2 * N_KV_HEADS):
            kvar_ref[i, 0:WINDOW, :] = jnp.zeros((WINDOW, PAIR_WIDTH), bf16)
            vvar_ref[i, 0:WINDOW, :] = jnp.zeros((WINDOW, PAIR_WIDTH), bf16)

    out = _dot(merged_ref[...], w_o_ref[...])
    o_ref[0] = (xkeep_ref[...]
                + out * lax.rsqrt(jnp.mean(out * out, axis=-1, keepdims=True) + EPS) * post_ref[...])

    x = x_ref[0]
    h_ref[...] = (x * lax.rsqrt(jnp.mean(x * x, axis=-1, keepdims=True) + EPS) * pre_ref[...]).astype(bf16)
    xkeep_ref[...] = x

    def proj(off, width):
        return _dot(h_ref[...], w_in_ref[:, off:off + width])

    q_ref[...] = (proj(Q_OFF, ATTN_WIDTH) * np.float32(HEAD_DIM ** -0.5)).astype(bf16)
    w_kv = w_in_ref[:, K_OFF:K_OFF + 2 * KV_WIDTH]
    kv = jnp.concatenate([_dot(h_ref[:TILE // 2, :], w_kv), _dot(h_ref[TILE // 2:, :], w_kv)], axis=0)
    low = lax.broadcasted_iota(jnp.int32, (TILE, PAIR_WIDTH), 1) < HEAD_DIM

    def place(var_ref, t):
        zero = jnp.zeros_like(t)
        swapped = pltpu.roll(t, HEAD_DIM, 1)
        var_ref[0, WINDOW:, :] = jnp.where(low, t, zero)
        var_ref[1, WINDOW:, :] = jnp.where(low, zero, swapped)
        var_ref[2, WINDOW:, :] = jnp.where(low, swapped, zero)
        var_ref[3, WINDOW:, :] = jnp.where(low, zero, t)

    place(kvar_ref, kv[:, :KV_WIDTH].astype(bf16))
    place(vvar_ref, kv[:, KV_WIDTH:].astype(bf16))

    qi = lax.broadcasted_iota(jnp.int32, (WINDOW, 2 * WINDOW), 0)
    kj = lax.broadcasted_iota(jnp.int32, (WINDOW, 2 * WINDOW), 1)
    dist = qi + WINDOW - kj
    in_window = jnp.logical_and(dist >= 0, dist < WINDOW)
    first_mask = jnp.logical_and(in_window, jnp.logical_or(kj >= WINDOW, t_idx > 0))
    band_low = lax.broadcasted_iota(jnp.int32, (2 * WINDOW, PAIR_WIDTH), 1) < HEAD_DIM
    ones_even = jnp.where(band_low, 1.0, 0.0).astype(bf16)
    ones_odd = jnp.where(band_low, 0.0, 1.0).astype(bf16)
    out_low = lax.broadcasted_iota(jnp.int32, (WINDOW, PAIR_WIDTH), 1) < HEAD_DIM

    def attn_scores(blk, j):
        r0 = blk * WINDOW
        q_g = jnp.concatenate(
            [q_ref[r0:r0 + WINDOW, (PAIRS_PER_KV * j + p) * PAIR_WIDTH:(PAIRS_PER_KV * j + p + 1) * PAIR_WIDTH]
             for p in range(PAIRS_PER_KV)], axis=0)
        k_g = jnp.concatenate([kvar_ref[2 * j + e, r0:r0 + 2 * WINDOW, :] for e in range(2)], axis=0)
        return lax.dot_general(q_g, k_g, (((1,), (1,)), ((), ())),
                               preferred_element_type=jnp.float32)

    def attn_softmax(blk, j, s_g):
        mask = first_mask if blk == 0 else in_window
        rows, sink_terms = [], []
        for p in range(PAIRS_PER_KV):
            probs, terms = [], []
            for e in range(2):
                hq = GQA_GROUP * j + 2 * p + e
                s = s_g[p * WINDOW:(p + 1) * WINDOW, e * 2 * WINDOW:(e + 1) * 2 * WINDOW]
                s = jnp.where(mask, s + bias_ref[hq], NEG_INF)
                sink = sinks_ref[hq]
                m = jnp.maximum(jnp.max(s, axis=-1, keepdims=True), sink)
                probs.append(jnp.exp(s - m).astype(bf16))
                terms.append(jnp.exp(sink - m))
            rows.append(jnp.concatenate(probs, axis=1))
            sink_terms.append(jnp.where(out_low, terms[0], terms[1]))
        return jnp.concatenate(rows, axis=0), sink_terms

    def attn_pv(blk, j, p_g, sink_terms):
        r0 = blk * WINDOW
        v_g = jnp.concatenate(
            [jnp.concatenate([vvar_ref[2 * j + e, r0:r0 + 2 * WINDOW, :], ones], axis=1)
             for e, ones in ((0, ones_even), (1, ones_odd))], axis=0)
        r_g = _dot(p_g, v_g)
        for p in range(PAIRS_PER_KV):
            r = r_g[p * WINDOW:(p + 1) * WINDOW]
            c0 = (PAIRS_PER_KV * j + p) * PAIR_WIDTH
            o_pair = r[:, :PAIR_WIDTH] / (r[:, PAIR_WIDTH:] + sink_terms[p])
            yain_ref[r0:r0 + WINDOW, c0:c0 + PAIR_WIDTH] = (
                o_pair * ga_ref[r0:r0 + WINDOW, c0:c0 + PAIR_WIDTH]).astype(bf16)

    def spatial_mix():
        n_chunks = TILE // CHUNK
        for g in range(SGU_GROUPS):
            c_lo = g * SGU_GROUP_DIM
            v_g = jnp.concatenate(
                [vn_ref[c * CHUNK:(c + 1) * CHUNK, c_lo:c_lo + SGU_GROUP_DIM] for c in range(n_chunks)], axis=1)
            m_g = _dot(wtril_ref[g], v_g) + b_sp_ref[g]
            for c in range(n_chunks):
                rows = slice(c * CHUNK, (c + 1) * CHUNK)
                cols = slice(c_lo, c_lo + SGU_GROUP_DIM)
                ybin_ref[rows, cols] = (m_g[:, c * SGU_GROUP_DIM:(c + 1) * SGU_GROUP_DIM]
                                        * t_ref[rows, cols]).astype(bf16)

    def job_v():
        v_act = _gelu_exact(proj(VB_OFF, SGU_WIDTH))
        mu = jnp.mean(v_act, axis=-1, keepdims=True)
        vc = v_act - mu
        vn_ref[...] = (vc * lax.rsqrt(jnp.mean(vc * vc, axis=-1, keepdims=True) + EPS) * lng_ref[...]
                       + lnb_ref[...]).astype(bf16)

    def job_u():
        gu_ref[...] = _gelu_exact(proj(U_OFF, SGU_WIDTH)).astype(bf16)

    def job_gate_b():
        t_ref[...] = (_silu(proj(GB_OFF, SGU_WIDTH)) * gu_ref[...]).astype(bf16)

    def job_merge_b():
        sb_ref[...] = jax.nn.sigmoid(proj(MB_OFF, D_MODEL)).astype(bf16)

    def job_branch_b():
        tb_ref[...] = _dot(ybin_ref[...], w_b_ref[...]) * sb_ref[...]

    def job_merge_a():
        sa_ref[...] = jax.nn.sigmoid(proj(MA_OFF, D_MODEL)).astype(bf16)

    dense_jobs = [job_v, job_u, job_gate_b, spatial_mix, job_merge_b, job_branch_b, job_merge_a]
    groups = [(blk, j) for blk in range(BLOCKS_PER_TILE) for j in range(N_KV_HEADS)]
    ga_ref[...] = _silu(proj(GA_OFF, ATTN_WIDTH)).astype(bf16)
    s_g = attn_scores(*groups[0])
    for gi, group in enumerate(groups):
        if dense_jobs:
            dense_jobs.pop(0)()
        p_g, sink_terms = attn_softmax(*group, s_g)
        if gi + 1 < len(groups):
            s_g = attn_scores(*groups[gi + 1])
        attn_pv(*group, p_g, sink_terms)

    for i in range(2 * N_KV_HEADS):
        kvar_ref[i, 0:WINDOW, :] = kvar_ref[i, TILE:TILE + WINDOW, :]
        vvar_ref[i, 0:WINDOW, :] = vvar_ref[i, TILE:TILE + WINDOW, :]

    for job in dense_jobs:
        job()
    merged_ref[...] = (_dot(yain_ref[...], w_a_ref[...]) * sa_ref[...] + tb_ref[...]).astype(bf16)


def _resident(shape):
    return pl.BlockSpec(shape, lambda s, *_: (0,) * len(shape), pipeline_mode=pl.Buffered(1))


def _layer(x, rel_bias, sinks, bucket, pre, post, lng, lnb, w_in, w_sp, b_sp, w_a, w_b, w_o):
    B, S, D = x.shape
    assert D == D_MODEL and S % TILE == 0 and w_in.shape == (D_MODEL, IN_COLS)
    tiles_per_seq = S // TILE
    n_tiles = B * tiles_per_seq

    def tile_index(tile):
        return (tile // tiles_per_seq, tile % tiles_per_seq, 0)

    x_spec = pl.BlockSpec((1, TILE, D), lambda s, *_: tile_index(jnp.minimum(s, n_tiles - 1)))
    o_spec = pl.BlockSpec((1, TILE, D), lambda s, *_: tile_index(jnp.maximum(s - 1, 0)))
    stage_bf16 = pltpu.VMEM((TILE, D), jnp.bfloat16)
    stage_f32 = pltpu.VMEM((TILE, D), jnp.float32)
    grid_spec = pltpu.PrefetchScalarGridSpec(
        num_scalar_prefetch=2,
        grid=(n_tiles + 1,),
        in_specs=[
            x_spec,
            _resident((WINDOW, 2 * WINDOW)),
            _resident((1, D)), _resident((1, D)),
            _resident((1, SGU_WIDTH)), _resident((1, SGU_WIDTH)),
            _resident((D, IN_COLS)),
            _resident((SGU_GROUPS, CHUNK, CHUNK)),
            _resident((SGU_GROUPS, CHUNK, 1)),
            _resident((ATTN_WIDTH, D)), _resident((SGU_WIDTH, D)), _resident((D, D)),
        ],
        out_specs=o_spec,
        scratch_shapes=[
            pltpu.VMEM((N_Q_HEADS, WINDOW, 2 * WINDOW), jnp.float32),
            pltpu.VMEM((SGU_GROUPS, CHUNK, CHUNK), jnp.bfloat16),
            pltpu.VMEM((2 * N_KV_HEADS, TILE + WINDOW, PAIR_WIDTH), jnp.bfloat16),
            pltpu.VMEM((2 * N_KV_HEADS, TILE + WINDOW, PAIR_WIDTH), jnp.bfloat16),
            stage_bf16,
            stage_f32,
            stage_bf16, stage_bf16,
            stage_bf16, stage_bf16,
            stage_bf16, stage_bf16, stage_bf16,
            stage_bf16, stage_bf16, stage_bf16,
            stage_f32,
        ],
    )
    return pl.pallas_call(
        functools.partial(_layer_kernel, tiles_per_seq=tiles_per_seq),
        out_shape=jax.ShapeDtypeStruct(x.shape, x.dtype),
        grid_spec=grid_spec,
        compiler_params=pltpu.CompilerParams(
            dimension_semantics=("arbitrary",),
            vmem_limit_bytes=V7X_VMEM_LIMIT_BYTES),
        name="hybrid_layer",
    )(rel_bias, sinks, x, bucket, pre, post, lng, lnb, w_in, w_sp, b_sp, w_a, w_b, w_o)


def kernel(x, rel_bias, pre_norm, post_norm, w_in, sinks, sgu_ln_g, sgu_ln_b, w_spatial, b_spatial,
           w_branch_a, w_branch_b, w_out):
    depth = w_in.shape[0]
    bucket = jnp.asarray(_t5_bucket_table())
    bf16 = jnp.bfloat16
    for l in range(depth):
        x = _layer(
            x, rel_bias, sinks[l], bucket,
            pre_norm[l][None], post_norm[l][None], sgu_ln_g[l][None], sgu_ln_b[l][None],
            w_in[l].astype(bf16), w_spatial[l], b_spatial[l][:, :, None],
            w_branch_a[l].astype(bf16), w_branch_b[l].astype(bf16), w_out[l].astype(bf16))
    return x
```

```python
import functools

import jax
import jax.numpy as jnp
import numpy as np
from jax import lax
from jax.experimental import pallas as pl
from jax.experimental.pallas import tpu as pltpu

D_MODEL = 1024
HEAD_DIM = 64
N_Q_HEADS = 16
N_KV_HEADS = 2
GQA_GROUP = N_Q_HEADS // N_KV_HEADS
WINDOW = 128
ATTN_WIDTH = N_Q_HEADS * HEAD_DIM
KV_WIDTH = N_KV_HEADS * HEAD_DIM
PAIR_WIDTH = 2 * HEAD_DIM
PAIRS_PER_KV = GQA_GROUP // 2
N_BUCKETS = 32
MAX_DISTANCE = 128
SGU_WIDTH = D_MODEL
SGU_GROUPS = 8
SGU_GROUP_DIM = SGU_WIDTH // SGU_GROUPS
CHUNK = 128
EPS = 1e-6
NEG_INF = -1e30
LOG2_E = float(np.log2(np.e))
BF16_SUBLANES = 16

_SPLITS = (ATTN_WIDTH, KV_WIDTH, KV_WIDTH, ATTN_WIDTH, SGU_WIDTH, SGU_WIDTH, SGU_WIDTH, D_MODEL, D_MODEL)
_OFFS = tuple(int(o) for o in np.cumsum((0,) + _SPLITS))
IN_COLS = _OFFS[-1]
(Q_OFF, K_OFF, V_OFF, GA_OFF, U_OFF, VB_OFF, GB_OFF, MA_OFF, MB_OFF) = _OFFS[:-1]

TILE = 256
BLOCKS_PER_TILE = TILE // WINDOW
V7X_VMEM_LIMIT_BYTES = 56 * 1024 * 1024


def _t5_bucket_table():
    qi = np.arange(WINDOW)[:, None]
    kj = np.arange(2 * WINDOW)[None, :]
    dist = np.clip(qi + WINDOW - kj, 0, None)
    max_exact = N_BUCKETS // 2
    large = max_exact + (np.log(np.maximum(dist, max_exact) / max_exact)
                         / np.log(MAX_DISTANCE / max_exact)
                         * (N_BUCKETS - max_exact)).astype(np.int32)
    large = np.minimum(large, N_BUCKETS - 1)
    return np.where(dist < max_exact, dist, large).astype(np.int32)


def _dot(a, b):
    return jnp.dot(a, b, preferred_element_type=jnp.float32)


def _gelu_exact(x):
    return 0.5 * x * (1.0 + lax.erf(x * np.float32(np.sqrt(0.5))))


def _silu(x):
    return x * jax.nn.sigmoid(x)


def _layer_kernel(rel_bias_ref, sinks_ref,
                  x_ref, bucket_ref, pre_ref, post_ref, lng_ref, lnb_ref,
                  w_in_ref, w_sp_ref, b_sp_ref, w_a_ref, w_b_ref, w_o_ref,
                  o_ref,
                  bias_ref, wtril_ref, kvar_ref, vvar_ref, merged_ref, xkeep_ref,
                  h_ref, q_ref, ga_ref, yain_ref, vn_ref, gu_ref, t_ref, ybin_ref, sb_ref, sa_ref, tb_ref,
                  *, tiles_per_seq, layer):
    step = pl.program_id(0)
    t_idx = jnp.minimum(step, pl.num_programs(0) - 2) % tiles_per_seq
    bf16 = jnp.bfloat16

    @pl.when(step == 0)
    def _():
        merged_ref[...] = jnp.zeros_like(merged_ref)
        xkeep_ref[...] = jnp.zeros_like(xkeep_ref)
        bucket = bucket_ref[...]
        sink_col = lax.broadcasted_iota(jnp.int32, (WINDOW, 2 * WINDOW), 1) == 0
        for h in range(N_Q_HEADS):
            acc = jnp.zeros((WINDOW, 2 * WINDOW), jnp.float32)
            for bk in range(N_BUCKETS):
                acc = jnp.where(bucket == bk, rel_bias_ref[bk, h], acc)
            bias_ref[h] = jnp.where(sink_col, sinks_ref[layer, h], acc) * LOG2_E
        row = lax.broadcasted_iota(jnp.int32, (CHUNK, CHUNK), 0)
        col = lax.broadcasted_iota(jnp.int32, (CHUNK, CHUNK), 1)
        for g in range(SGU_GROUPS):
            wtril_ref[g] = jnp.where(col <= row, w_sp_ref[g], 0.0).astype(bf16)

    @pl.when(t_idx == 0)
    def _():
        for i in range(2 * N_KV_HEADS):
            kvar_ref[i, 0:WINDOW, :] = jnp.zeros((WINDOW, PAIR_WIDTH), bf16)
            vvar_ref[i, 0:WINDOW, :] = jnp.zeros((WINDOW, PAIR_WIDTH), bf16)

    out = _dot(merged_ref[...], w_o_ref[...])
    o_ref[0] = (xkeep_ref[...]
                + out * lax.rsqrt(jnp.mean(out * out, axis=-1, keepdims=True) + EPS) * post_ref[...])

    x = x_ref[0]
    h_ref[...] = (x * lax.rsqrt(jnp.mean(x * x, axis=-1, keepdims=True) + EPS) * pre_ref[...]).astype(bf16)
    xkeep_ref[...] = x

    def proj(off, width):
        return _dot(h_ref[...], w_in_ref[:, off:off + width])

    q_ref[...] = (proj(Q_OFF, ATTN_WIDTH) * np.float32(HEAD_DIM ** -0.5 * LOG2_E)).astype(bf16)
    w_kv = w_in_ref[:, K_OFF:K_OFF + 2 * KV_WIDTH]
    kv = jnp.concatenate([_dot(h_ref[:TILE // 2, :], w_kv), _dot(h_ref[TILE // 2:, :], w_kv)], axis=0)
    low = lax.broadcasted_iota(jnp.int32, (TILE, PAIR_WIDTH), 1) < HEAD_DIM

    def place(var_ref, t):
        zero = jnp.zeros_like(t)
        swapped = pltpu.roll(t, HEAD_DIM, 1)
        var_ref[0, WINDOW:, :] = jnp.where(low, t, zero)
        var_ref[1, WINDOW:, :] = jnp.where(low, zero, swapped)
        var_ref[2, WINDOW:, :] = jnp.where(low, swapped, zero)
        var_ref[3, WINDOW:, :] = jnp.where(low, zero, t)

    place(kvar_ref, kv[:, :KV_WIDTH].astype(bf16))
    place(vvar_ref, kv[:, KV_WIDTH:].astype(bf16))

    qi = lax.broadcasted_iota(jnp.int32, (WINDOW, 2 * WINDOW), 0)
    kj = lax.broadcasted_iota(jnp.int32, (WINDOW, 2 * WINDOW), 1)
    dist = qi + WINDOW - kj
    in_window = jnp.logical_or(jnp.logical_and(dist >= 0, dist < WINDOW), kj == 0)
    first_mask = jnp.logical_and(in_window, jnp.logical_or(jnp.logical_or(kj >= WINDOW, kj == 0), t_idx > 0))
    band_low = lax.broadcasted_iota(jnp.int32, (2 * WINDOW, PAIR_WIDTH), 1) < HEAD_DIM
    ones_even = jnp.where(band_low, 1.0, 0.0).astype(bf16)
    ones_odd = jnp.where(band_low, 0.0, 1.0).astype(bf16)
    first_row = lax.broadcasted_iota(jnp.int32, (BF16_SUBLANES, PAIR_WIDTH), 0) == 0

    def band(var_ref, variant, r0):
        head = var_ref[variant, r0:r0 + BF16_SUBLANES, :]
        head = jnp.where(first_row, jnp.zeros_like(head), head)
        return jnp.concatenate([head, var_ref[variant, r0 + BF16_SUBLANES:r0 + 2 * WINDOW, :]], axis=0)

    def attn_scores(blk, j):
        r0 = blk * WINDOW
        q_g = jnp.concatenate(
            [q_ref[r0:r0 + WINDOW, (PAIRS_PER_KV * j + p) * PAIR_WIDTH:(PAIRS_PER_KV * j + p + 1) * PAIR_WIDTH]
             for p in range(PAIRS_PER_KV)], axis=0)
        k_g = jnp.concatenate([band(kvar_ref, 2 * j + e, r0) for e in range(2)], axis=0)
        return lax.dot_general(q_g, k_g, (((1,), (1,)), ((), ())),
                               preferred_element_type=jnp.float32)

    def attn_softmax(blk, j, s_g):
        mask = first_mask if blk == 0 else in_window
        rows = []
        for p in range(PAIRS_PER_KV):
            probs = []
            for e in range(2):
                hq = GQA_GROUP * j + 2 * p + e
                s = s_g[p * WINDOW:(p + 1) * WINDOW, e * 2 * WINDOW:(e + 1) * 2 * WINDOW]
                s = jnp.where(mask, s + bias_ref[hq], NEG_INF)
                m = jnp.max(s, axis=-1, keepdims=True)
                probs.append(jnp.exp2(s - m).astype(bf16))
            rows.append(jnp.concatenate(probs, axis=1))
        return jnp.concatenate(rows, axis=0)

    def attn_pv(blk, j, p_g):
        r0 = blk * WINDOW
        v_g = jnp.concatenate(
            [jnp.concatenate([band(vvar_ref, 2 * j + e, r0), ones], axis=1)
             for e, ones in ((0, ones_even), (1, ones_odd))], axis=0)
        r_g = _dot(p_g, v_g)
        for p in range(PAIRS_PER_KV):
            r = r_g[p * WINDOW:(p + 1) * WINDOW]
            c0 = (PAIRS_PER_KV * j + p) * PAIR_WIDTH
            o_pair = r[:, :PAIR_WIDTH] / r[:, PAIR_WIDTH:]
            yain_ref[r0:r0 + WINDOW, c0:c0 + PAIR_WIDTH] = (
                o_pair * ga_ref[r0:r0 + WINDOW, c0:c0 + PAIR_WIDTH]).astype(bf16)

    def spatial_mix():
        n_chunks = TILE // CHUNK
        for g in range(SGU_GROUPS):
            c_lo = g * SGU_GROUP_DIM
            v_g = jnp.concatenate(
                [vn_ref[c * CHUNK:(c + 1) * CHUNK, c_lo:c_lo + SGU_GROUP_DIM] for c in range(n_chunks)], axis=1)
            m_g = _dot(wtril_ref[g], v_g) + b_sp_ref[g]
            for c in range(n_chunks):
                rows = slice(c * CHUNK, (c + 1) * CHUNK)
                cols = slice(c_lo, c_lo + SGU_GROUP_DIM)
                ybin_ref[rows, cols] = (m_g[:, c * SGU_GROUP_DIM:(c + 1) * SGU_GROUP_DIM]
                                        * t_ref[rows, cols]).astype(bf16)

    def job_v():
        v_act = _gelu_exact(proj(VB_OFF, SGU_WIDTH))
        mu = jnp.mean(v_act, axis=-1, keepdims=True)
        vc = v_act - mu
        vn_ref[...] = (vc * lax.rsqrt(jnp.mean(vc * vc, axis=-1, keepdims=True) + EPS) * lng_ref[...]
                       + lnb_ref[...]).astype(bf16)

    def job_u():
        gu_ref[...] = _gelu_exact(proj(U_OFF, SGU_WIDTH)).astype(bf16)

    def job_gate_b():
        t_ref[...] = (_silu(proj(GB_OFF, SGU_WIDTH)) * gu_ref[...]).astype(bf16)

    def job_merge_b():
        sb_ref[...] = jax.nn.sigmoid(proj(MB_OFF, D_MODEL)).astype(bf16)

    def job_branch_b():
        tb_ref[...] = _dot(ybin_ref[...], w_b_ref[...]) * sb_ref[...]

    def job_merge_a():
        sa_ref[...] = jax.nn.sigmoid(proj(MA_OFF, D_MODEL)).astype(bf16)

    dense_jobs = [job_v, job_u, job_gate_b, spatial_mix, job_merge_b, job_branch_b, job_merge_a]
    groups = [(blk, j) for blk in range(BLOCKS_PER_TILE) for j in range(N_KV_HEADS)]
    ga_ref[...] = _silu(proj(GA_OFF, ATTN_WIDTH)).astype(bf16)
    s_g = attn_scores(*groups[0])
    for gi, group in enumerate(groups):
        if dense_jobs:
            dense_jobs.pop(0)()
        p_g = attn_softmax(*group, s_g)
        if gi + 1 < len(groups):
            s_g = attn_scores(*groups[gi + 1])
        attn_pv(*group, p_g)

    for i in range(2 * N_KV_HEADS):
        kvar_ref[i, 0:WINDOW, :] = kvar_ref[i, TILE:TILE + WINDOW, :]
        vvar_ref[i, 0:WINDOW, :] = vvar_ref[i, TILE:TILE + WINDOW, :]

    for job in dense_jobs:
        job()
    merged_ref[...] = (_dot(yain_ref[...], w_a_ref[...]) * sa_ref[...] + tb_ref[...]).astype(bf16)


def _resident(shape):
    return pl.BlockSpec(shape, lambda s, *_: (0,) * len(shape), pipeline_mode=pl.Buffered(1))


def _layer_slice(layer, shape):
    return pl.BlockSpec((None,) + shape, lambda s, *_: (layer,) + (0,) * len(shape),
                        pipeline_mode=pl.Buffered(1))


def _layer(layer, x, rel_bias, sinks, bucket, pre, post, lng, lnb, w_in, w_sp, b_sp, w_a, w_b, w_o):
    B, S, D = x.shape
    assert D == D_MODEL and S % TILE == 0 and w_in.shape[1:] == (D_MODEL, IN_COLS)
    tiles_per_seq = S // TILE
    n_tiles = B * tiles_per_seq

    def tile_index(tile):
        return (tile // tiles_per_seq, tile % tiles_per_seq, 0)

    x_spec = pl.BlockSpec((1, TILE, D), lambda s, *_: tile_index(jnp.minimum(s, n_tiles - 1)))
    o_spec = pl.BlockSpec((1, TILE, D), lambda s, *_: tile_index(jnp.maximum(s - 1, 0)))
    stage_bf16 = pltpu.VMEM((TILE, D), jnp.bfloat16)
    stage_f32 = pltpu.VMEM((TILE, D), jnp.float32)
    grid_spec = pltpu.PrefetchScalarGridSpec(
        num_scalar_prefetch=2,
        grid=(n_tiles + 1,),
        in_specs=[
            x_spec,
            _resident((WINDOW, 2 * WINDOW)),
            _layer_slice(layer, (1, D)), _layer_slice(layer, (1, D)),
            _layer_slice(layer, (1, SGU_WIDTH)), _layer_slice(layer, (1, SGU_WIDTH)),
            _layer_slice(layer, (D, IN_COLS)),
            _layer_slice(layer, (SGU_GROUPS, CHUNK, CHUNK)),
            _layer_slice(layer, (SGU_GROUPS, CHUNK, 1)),
            _layer_slice(layer, (ATTN_WIDTH, D)), _layer_slice(layer, (SGU_WIDTH, D)),
            _layer_slice(layer, (D, D)),
        ],
        out_specs=o_spec,
        scratch_shapes=[
            pltpu.VMEM((N_Q_HEADS, WINDOW, 2 * WINDOW), jnp.float32),
            pltpu.VMEM((SGU_GROUPS, CHUNK, CHUNK), jnp.bfloat16),
            pltpu.VMEM((2 * N_KV_HEADS, TILE + WINDOW, PAIR_WIDTH), jnp.bfloat16),
            pltpu.VMEM((2 * N_KV_HEADS, TILE + WINDOW, PAIR_WIDTH), jnp.bfloat16),
            stage_bf16,
            stage_f32,
            stage_bf16, stage_bf16,
            stage_bf16, stage_bf16,
            stage_bf16, stage_bf16, stage_bf16,
            stage_bf16, stage_bf16, stage_bf16,
            stage_f32,
        ],
    )
    return pl.pallas_call(
        functools.partial(_layer_kernel, tiles_per_seq=tiles_per_seq, layer=layer),
        out_shape=jax.ShapeDtypeStruct(x.shape, x.dtype),
        grid_spec=grid_spec,
        compiler_params=pltpu.CompilerParams(
            dimension_semantics=("arbitrary",),
            vmem_limit_bytes=V7X_VMEM_LIMIT_BYTES),
        name="hybrid_layer",
    )(rel_bias, sinks, x, bucket, pre, post, lng, lnb, w_in, w_sp, b_sp, w_a, w_b, w_o)


def kernel(x, rel_bias, pre_norm, post_norm, w_in, sinks, sgu_ln_g, sgu_ln_b, w_spatial, b_spatial,
           w_branch_a, w_branch_b, w_out):
    depth = w_in.shape[0]
    bucket = jnp.asarray(_t5_bucket_table())
    bf16 = jnp.bfloat16
    params = (rel_bias, sinks, bucket,
              pre_norm[:, None], post_norm[:, None], sgu_ln_g[:, None], sgu_ln_b[:, None],
              w_in.astype(bf16), w_spatial, b_spatial[..., None],
              w_branch_a.astype(bf16), w_branch_b.astype(bf16), w_out.astype(bf16))
    for l in range(depth):
        x = _layer(l, x, *params)
    return x
```

```python
import functools

import jax
import jax.numpy as jnp
import numpy as np
from jax import lax
from jax.experimental import pallas as pl
from jax.experimental.pallas import tpu as pltpu

D_MODEL = 1024
HEAD_DIM = 64
N_Q_HEADS = 16
N_KV_HEADS = 2
GQA_GROUP = N_Q_HEADS // N_KV_HEADS
WINDOW = 128
ATTN_WIDTH = N_Q_HEADS * HEAD_DIM
KV_WIDTH = N_KV_HEADS * HEAD_DIM
PAIR_WIDTH = 2 * HEAD_DIM
PAIRS_PER_KV = GQA_GROUP // 2
N_BUCKETS = 32
MAX_DISTANCE = 128
SGU_WIDTH = D_MODEL
SGU_GROUPS = 8
SGU_GROUP_DIM = SGU_WIDTH // SGU_GROUPS
CHUNK = 128
EPS = 1e-6
NEG_INF = -1e30
LOG2_E = float(np.log2(np.e))
BF16_SUBLANES = 16

_SPLITS = (ATTN_WIDTH, KV_WIDTH, KV_WIDTH, ATTN_WIDTH, SGU_WIDTH, SGU_WIDTH, SGU_WIDTH, D_MODEL, D_MODEL)
_OFFS = tuple(int(o) for o in np.cumsum((0,) + _SPLITS))
IN_COLS = _OFFS[-1]
(Q_OFF, K_OFF, V_OFF, GA_OFF, U_OFF, VB_OFF, GB_OFF, MA_OFF, MB_OFF) = _OFFS[:-1]

TILE = 256
BLOCKS_PER_TILE = TILE // WINDOW
V7X_VMEM_LIMIT_BYTES = 56 * 1024 * 1024


def _t5_bucket_table():
    qi = np.arange(WINDOW)[:, None]
    kj = np.arange(2 * WINDOW)[None, :]
    dist = np.clip(qi + WINDOW - kj, 0, None)
    max_exact = N_BUCKETS // 2
    large = max_exact + (np.log(np.maximum(dist, max_exact) / max_exact)
                         / np.log(MAX_DISTANCE / max_exact)
                         * (N_BUCKETS - max_exact)).astype(np.int32)
    large = np.minimum(large, N_BUCKETS - 1)
    return np.where(dist < max_exact, dist, large).astype(np.int32)


def _dot(a, b):
    return jnp.dot(a, b, preferred_element_type=jnp.float32)


def _gelu_exact(x):
    return 0.5 * x * (1.0 + lax.erf(x * np.float32(np.sqrt(0.5))))


def _sigmoid(x):
    return 0.5 * (1.0 + jnp.tanh(0.5 * x))


def _silu(x):
    half = 0.5 * x
    return half * (1.0 + jnp.tanh(half))


def _layer_kernel(rel_bias_ref, sinks_ref,
                  x_ref, xprev_ref, bucket_ref, pre_ref, post_ref, lng_ref, lnb_ref,
                  w_in_ref, w_sp_ref, b_sp_ref, w_a_ref, w_b_ref, w_o_ref,
                  o_ref,
                  bias_ref, wtril_ref, kvar_ref, vvar_ref, merged_ref,
                  h_ref, q_ref, ga_ref, yain_ref, vn_ref, gu_ref, t_ref, ybin_ref, sb_ref, sa_ref, tb_ref,
                  *, tiles_per_seq, layer):
    step = pl.program_id(0)
    t_idx = jnp.minimum(step, pl.num_programs(0) - 2) % tiles_per_seq
    bf16 = jnp.bfloat16

    @pl.when(step == 0)
    def _():
        merged_ref[...] = jnp.zeros_like(merged_ref)
        bucket = bucket_ref[...]
        sink_col = lax.broadcasted_iota(jnp.int32, (WINDOW, 2 * WINDOW), 1) == 0
        for h in range(N_Q_HEADS):
            acc = jnp.zeros((WINDOW, 2 * WINDOW), jnp.float32)
            for bk in range(N_BUCKETS):
                acc = jnp.where(bucket == bk, rel_bias_ref[bk, h], acc)
            bias_ref[h] = jnp.where(sink_col, sinks_ref[layer, h], acc) * LOG2_E
        row = lax.broadcasted_iota(jnp.int32, (CHUNK, CHUNK), 0)
        col = lax.broadcasted_iota(jnp.int32, (CHUNK, CHUNK), 1)
        for g in range(SGU_GROUPS):
            wtril_ref[g] = jnp.where(col <= row, w_sp_ref[g], 0.0).astype(bf16)

    @pl.when(t_idx == 0)
    def _():
        for i in range(2 * N_KV_HEADS):
            kvar_ref[i, 0:WINDOW, :] = jnp.zeros((WINDOW, PAIR_WIDTH), bf16)
            vvar_ref[i, 0:WINDOW, :] = jnp.zeros((WINDOW, PAIR_WIDTH), bf16)

    out = _dot(merged_ref[...], w_o_ref[...])
    o_ref[0] = (xprev_ref[0]
                + out * lax.rsqrt(jnp.mean(out * out, axis=-1, keepdims=True) + EPS) * post_ref[...])

    x = x_ref[0]
    h_ref[...] = (x * lax.rsqrt(jnp.mean(x * x, axis=-1, keepdims=True) + EPS) * pre_ref[...]).astype(bf16)

    def proj(off, width):
        return _dot(h_ref[...], w_in_ref[:, off:off + width])

    q_ref[...] = (proj(Q_OFF, ATTN_WIDTH) * np.float32(HEAD_DIM ** -0.5 * LOG2_E)).astype(bf16)
    w_kv = w_in_ref[:, K_OFF:K_OFF + 2 * KV_WIDTH]
    kv = jnp.concatenate([_dot(h_ref[:TILE // 2, :], w_kv), _dot(h_ref[TILE // 2:, :], w_kv)], axis=0)
    low = lax.broadcasted_iota(jnp.int32, (TILE, PAIR_WIDTH), 1) < HEAD_DIM

    def place(var_ref, t):
        zero = jnp.zeros_like(t)
        swapped = pltpu.roll(t, HEAD_DIM, 1)
        var_ref[0, WINDOW:, :] = jnp.where(low, t, zero)
        var_ref[1, WINDOW:, :] = jnp.where(low, zero, swapped)
        var_ref[2, WINDOW:, :] = jnp.where(low, swapped, zero)
        var_ref[3, WINDOW:, :] = jnp.where(low, zero, t)

    place(kvar_ref, kv[:, :KV_WIDTH].astype(bf16))
    place(vvar_ref, kv[:, KV_WIDTH:].astype(bf16))

    qi = lax.broadcasted_iota(jnp.int32, (WINDOW, 2 * WINDOW), 0)
    kj = lax.broadcasted_iota(jnp.int32, (WINDOW, 2 * WINDOW), 1)
    dist = qi + WINDOW - kj
    in_window = jnp.logical_or(jnp.logical_and(dist >= 0, dist < WINDOW), kj == 0)
    first_mask = jnp.logical_and(in_window, jnp.logical_or(jnp.logical_or(kj >= WINDOW, kj == 0), t_idx > 0))
    band_low = lax.broadcasted_iota(jnp.int32, (2 * WINDOW, PAIR_WIDTH), 1) < HEAD_DIM
    ones_even = jnp.where(band_low, 1.0, 0.0).astype(bf16)
    ones_odd = jnp.where(band_low, 0.0, 1.0).astype(bf16)
    first_row = lax.broadcasted_iota(jnp.int32, (BF16_SUBLANES, PAIR_WIDTH), 0) == 0

    def band(var_ref, variant, r0):
        head = var_ref[variant, r0:r0 + BF16_SUBLANES, :]
        head = jnp.where(first_row, jnp.zeros_like(head), head)
        return jnp.concatenate([head, var_ref[variant, r0 + BF16_SUBLANES:r0 + 2 * WINDOW, :]], axis=0)

    def attn_scores(blk, j):
        r0 = blk * WINDOW
        q_g = jnp.concatenate(
            [q_ref[r0:r0 + WINDOW, (PAIRS_PER_KV * j + p) * PAIR_WIDTH:(PAIRS_PER_KV * j + p + 1) * PAIR_WIDTH]
             for p in range(PAIRS_PER_KV)], axis=0)
        k_g = jnp.concatenate([band(kvar_ref, 2 * j + e, r0) for e in range(2)], axis=0)
        return lax.dot_general(q_g, k_g, (((1,), (1,)), ((), ())),
                               preferred_element_type=jnp.float32)

    def attn_softmax(blk, j, s_g):
        mask = first_mask if blk == 0 else in_window
        rows = []
        for p in range(PAIRS_PER_KV):
            probs = []
            for e in range(2):
                hq = GQA_GROUP * j + 2 * p + e
                s = s_g[p * WINDOW:(p + 1) * WINDOW, e * 2 * WINDOW:(e + 1) * 2 * WINDOW]
                s = jnp.where(mask, s + bias_ref[hq], NEG_INF)
                m = jnp.max(s, axis=-1, keepdims=True)
                probs.append(jnp.exp2(s - m).astype(bf16))
            rows.append(jnp.concatenate(probs, axis=1))
        return jnp.concatenate(rows, axis=0)

    def attn_pv(blk, j, p_g):
        r0 = blk * WINDOW
        v_g = jnp.concatenate(
            [jnp.concatenate([band(vvar_ref, 2 * j + e, r0), ones], axis=1)
             for e, ones in ((0, ones_even), (1, ones_odd))], axis=0)
        r_g = _dot(p_g, v_g)
        for p in range(PAIRS_PER_KV):
            r = r_g[p * WINDOW:(p + 1) * WINDOW]
            c0 = (PAIRS_PER_KV * j + p) * PAIR_WIDTH
            o_pair = r[:, :PAIR_WIDTH] / r[:, PAIR_WIDTH:]
            yain_ref[r0:r0 + WINDOW, c0:c0 + PAIR_WIDTH] = (
                o_pair * ga_ref[r0:r0 + WINDOW, c0:c0 + PAIR_WIDTH]).astype(bf16)

    def spatial_mix():
        n_chunks = TILE // CHUNK
        for g in range(SGU_GROUPS):
            c_lo = g * SGU_GROUP_DIM
            v_g = jnp.concatenate(
                [vn_ref[c * CHUNK:(c + 1) * CHUNK, c_lo:c_lo + SGU_GROUP_DIM] for c in range(n_chunks)], axis=1)
            m_g = _dot(wtril_ref[g], v_g) + b_sp_ref[g]
            for c in range(n_chunks):
                rows = slice(c * CHUNK, (c + 1) * CHUNK)
                cols = slice(c_lo, c_lo + SGU_GROUP_DIM)
                ybin_ref[rows, cols] = (m_g[:, c * SGU_GROUP_DIM:(c + 1) * SGU_GROUP_DIM]
                                        * t_ref[rows, cols]).astype(bf16)

    def job_v():
        v_act = _gelu_exact(proj(VB_OFF, SGU_WIDTH))
        mu = jnp.mean(v_act, axis=-1, keepdims=True)
        vc = v_act - mu
        vn_ref[...] = (vc * lax.rsqrt(jnp.mean(vc * vc, axis=-1, keepdims=True) + EPS) * lng_ref[...]
                       + lnb_ref[...]).astype(bf16)

    def job_u():
        gu_ref[...] = _gelu_exact(proj(U_OFF, SGU_WIDTH)).astype(bf16)

    def job_gate_b():
        t_ref[...] = (_silu(proj(GB_OFF, SGU_WIDTH)) * gu_ref[...]).astype(bf16)

    def job_merge_b():
        sb_ref[...] = _sigmoid(proj(MB_OFF, D_MODEL)).astype(bf16)

    def job_branch_b():
        tb_ref[...] = _dot(ybin_ref[...], w_b_ref[...]) * sb_ref[...]

    def job_merge_a():
        sa_ref[...] = _sigmoid(proj(MA_OFF, D_MODEL)).astype(bf16)

    dense_jobs = [job_v, job_u, job_gate_b, spatial_mix, job_merge_b, job_branch_b, job_merge_a]
    groups = [(blk, j) for blk in range(BLOCKS_PER_TILE) for j in range(N_KV_HEADS)]
    ga_ref[...] = _silu(proj(GA_OFF, ATTN_WIDTH)).astype(bf16)
    s_g = attn_scores(*groups[0])
    for gi, group in enumerate(groups):
        if dense_jobs:
            dense_jobs.pop(0)()
        p_g = attn_softmax(*group, s_g)
        if gi + 1 < len(groups):
            s_g = attn_scores(*groups[gi + 1])
        attn_pv(*group, p_g)

    for i in range(2 * N_KV_HEADS):
        kvar_ref[i, 0:WINDOW, :] = kvar_ref[i, TILE:TILE + WINDOW, :]
        vvar_ref[i, 0:WINDOW, :] = vvar_ref[i, TILE:TILE + WINDOW, :]

    for job in dense_jobs:
        job()
    merged_ref[...] = (_dot(yain_ref[...], w_a_ref[...]) * sa_ref[...] + tb_ref[...]).astype(bf16)


def _resident(shape):
    return pl.BlockSpec(shape, lambda s, *_: (0,) * len(shape), pipeline_mode=pl.Buffered(1))


def _layer_slice(layer, shape):
    return pl.BlockSpec((None,) + shape, lambda s, *_: (layer,) + (0,) * len(shape),
                        pipeline_mode=pl.Buffered(1))


def _layer(layer, x, rel_bias, sinks, bucket, pre, post, lng, lnb, w_in, w_sp, b_sp, w_a, w_b, w_o):
    B, S, D = x.shape
    assert D == D_MODEL and S % TILE == 0 and w_in.shape[1:] == (D_MODEL, IN_COLS)
    tiles_per_seq = S // TILE
    n_tiles = B * tiles_per_seq

    def tile_index(tile):
        return (tile // tiles_per_seq, tile % tiles_per_seq, 0)

    x_spec = pl.BlockSpec((1, TILE, D), lambda s, *_: tile_index(jnp.minimum(s, n_tiles - 1)))
    o_spec = pl.BlockSpec((1, TILE, D), lambda s, *_: tile_index(jnp.maximum(s - 1, 0)))
    stage_bf16 = pltpu.VMEM((TILE, D), jnp.bfloat16)
    stage_f32 = pltpu.VMEM((TILE, D), jnp.float32)
    grid_spec = pltpu.PrefetchScalarGridSpec(
        num_scalar_prefetch=2,
        grid=(n_tiles + 1,),
        in_specs=[
            x_spec,
            o_spec,
            _resident((WINDOW, 2 * WINDOW)),
            _layer_slice(layer, (1, D)), _layer_slice(layer, (1, D)),
            _layer_slice(layer, (1, SGU_WIDTH)), _layer_slice(layer, (1, SGU_WIDTH)),
            _layer_slice(layer, (D, IN_COLS)),
            _layer_slice(layer, (SGU_GROUPS, CHUNK, CHUNK)),
            _layer_slice(layer, (SGU_GROUPS, CHUNK, 1)),
            _layer_slice(layer, (ATTN_WIDTH, D)), _layer_slice(layer, (SGU_WIDTH, D)),
            _layer_slice(layer, (D, D)),
        ],
        out_specs=o_spec,
        scratch_shapes=[
            pltpu.VMEM((N_Q_HEADS, WINDOW, 2 * WINDOW), jnp.float32),
            pltpu.VMEM((SGU_GROUPS, CHUNK, CHUNK), jnp.bfloat16),
            pltpu.VMEM((2 * N_KV_HEADS, TILE + WINDOW, PAIR_WIDTH), jnp.bfloat16),
            pltpu.VMEM((2 * N_KV_HEADS, TILE + WINDOW, PAIR_WIDTH), jnp.bfloat16),
            stage_bf16,
            stage_bf16, stage_bf16,
            stage_bf16, stage_bf16,
            stage_bf16, stage_bf16, stage_bf16,
            stage_bf16, stage_bf16, stage_bf16,
            stage_f32,
        ],
    )
    return pl.pallas_call(
        functools.partial(_layer_kernel, tiles_per_seq=tiles_per_seq, layer=layer),
        out_shape=jax.ShapeDtypeStruct(x.shape, x.dtype),
        grid_spec=grid_spec,
        compiler_params=pltpu.CompilerParams(
            dimension_semantics=("arbitrary",),
            vmem_limit_bytes=V7X_VMEM_LIMIT_BYTES),
        name="hybrid_layer",
    )(rel_bias, sinks, x, x, bucket, pre, post, lng, lnb, w_in, w_sp, b_sp, w_a, w_b, w_o)


def kernel(x, rel_bias, pre_norm, post_norm, w_in, sinks, sgu_ln_g, sgu_ln_b, w_spatial, b_spatial,
           w_branch_a, w_branch_b, w_out):
    depth = w_in.shape[0]
    bucket = jnp.asarray(_t5_bucket_table())
    bf16 = jnp.bfloat16
    params = (rel_bias, sinks, bucket,
              pre_norm[:, None], post_norm[:, None], sgu_ln_g[:, None], sgu_ln_b[:, None],
              w_in.astype(bf16), w_spatial, b_spatial[..., None],
              w_branch_a.astype(bf16), w_branch_b.astype(bf16), w_out.astype(bf16))
    for l in range(depth):
        x = _layer(l, x, *params)
    return x
```

```python
import functools

import jax
import jax.numpy as jnp
import numpy as np
from jax import lax
from jax.experimental import pallas as pl
from jax.experimental.pallas import tpu as pltpu

D_MODEL = 1024
HEAD_DIM = 64
N_Q_HEADS = 16
N_KV_HEADS = 2
GQA_GROUP = N_Q_HEADS // N_KV_HEADS
WINDOW = 128
ATTN_WIDTH = N_Q_HEADS * HEAD_DIM
KV_WIDTH = N_KV_HEADS * HEAD_DIM
PAIR_WIDTH = 2 * HEAD_DIM
PAIRS_PER_KV = GQA_GROUP // 2
N_BUCKETS = 32
MAX_DISTANCE = 128
SGU_WIDTH = D_MODEL
SGU_GROUPS = 8
SGU_GROUP_DIM = SGU_WIDTH // SGU_GROUPS
CHUNK = 128
EPS = 1e-6
NEG_INF = -1e30
LOG2_E = float(np.log2(np.e))
BF16_SUBLANES = 16

_SPLITS = (ATTN_WIDTH, KV_WIDTH, KV_WIDTH, ATTN_WIDTH, SGU_WIDTH, SGU_WIDTH, SGU_WIDTH, D_MODEL, D_MODEL)
_OFFS = tuple(int(o) for o in np.cumsum((0,) + _SPLITS))
IN_COLS = _OFFS[-1]
(Q_OFF, K_OFF, V_OFF, GA_OFF, U_OFF, VB_OFF, GB_OFF, MA_OFF, MB_OFF) = _OFFS[:-1]

TILE = 512
BLOCKS_PER_TILE = TILE // WINDOW
V7X_VMEM_LIMIT_BYTES = 56 * 1024 * 1024


def _t5_bucket_table():
    qi = np.arange(WINDOW)[:, None]
    kj = np.arange(2 * WINDOW)[None, :]
    dist = np.clip(qi + WINDOW - kj, 0, None)
    max_exact = N_BUCKETS // 2
    large = max_exact + (np.log(np.maximum(dist, max_exact) / max_exact)
                         / np.log(MAX_DISTANCE / max_exact)
                         * (N_BUCKETS - max_exact)).astype(np.int32)
    large = np.minimum(large, N_BUCKETS - 1)
    return np.where(dist < max_exact, dist, large).astype(np.int32)


def _dot(a, b):
    return jnp.dot(a, b, preferred_element_type=jnp.float32)


def _gelu_exact(x):
    return 0.5 * x * (1.0 + lax.erf(x * np.float32(np.sqrt(0.5))))


def _sigmoid(x):
    return 0.5 * (1.0 + jnp.tanh(0.5 * x))


def _silu(x):
    half = 0.5 * x
    return half * (1.0 + jnp.tanh(half))


def _layer_kernel(rel_bias_ref, sinks_ref,
                  x_ref, xprev_ref, bucket_ref, pre_ref, post_ref, lng_ref, lnb_ref,
                  w_in_ref, w_sp_ref, b_sp_ref, w_a_ref, w_b_ref, w_o_ref,
                  o_ref,
                  bias_ref, wtril_ref, kvar_ref, vvar_ref, merged_ref,
                  h_ref, q_ref, ga_ref, yain_ref, vn_ref, gu_ref, t_ref, ybin_ref, sb_ref, sa_ref, tb_ref,
                  *, tiles_per_seq, layer):
    step = pl.program_id(0)
    t_idx = jnp.minimum(step, pl.num_programs(0) - 2) % tiles_per_seq
    bf16 = jnp.bfloat16

    @pl.when(step == 0)
    def _():
        merged_ref[...] = jnp.zeros_like(merged_ref)
        bucket = bucket_ref[...]
        sink_col = lax.broadcasted_iota(jnp.int32, (WINDOW, 2 * WINDOW), 1) == 0
        for h in range(N_Q_HEADS):
            acc = jnp.zeros((WINDOW, 2 * WINDOW), jnp.float32)
            for bk in range(N_BUCKETS):
                acc = jnp.where(bucket == bk, rel_bias_ref[bk, h], acc)
            bias_ref[h] = jnp.where(sink_col, sinks_ref[layer, h], acc) * LOG2_E
        row = lax.broadcasted_iota(jnp.int32, (CHUNK, CHUNK), 0)
        col = lax.broadcasted_iota(jnp.int32, (CHUNK, CHUNK), 1)
        for g in range(SGU_GROUPS):
            wtril_ref[g] = jnp.where(col <= row, w_sp_ref[g], 0.0).astype(bf16)

    @pl.when(t_idx == 0)
    def _():
        for i in range(2 * N_KV_HEADS):
            kvar_ref[i, 0:WINDOW, :] = jnp.zeros((WINDOW, PAIR_WIDTH), bf16)
            vvar_ref[i, 0:WINDOW, :] = jnp.zeros((WINDOW, PAIR_WIDTH), bf16)

    out = _dot(merged_ref[...], w_o_ref[...])
    o_ref[0] = (xprev_ref[0]
                + out * lax.rsqrt(jnp.mean(out * out, axis=-1, keepdims=True) + EPS) * post_ref[...])

    x = x_ref[0]
    h_ref[...] = (x * lax.rsqrt(jnp.mean(x * x, axis=-1, keepdims=True) + EPS) * pre_ref[...]).astype(bf16)

    def proj(off, width):
        return _dot(h_ref[...], w_in_ref[:, off:off + width])

    q_ref[...] = (proj(Q_OFF, ATTN_WIDTH) * np.float32(HEAD_DIM ** -0.5 * LOG2_E)).astype(bf16)
    w_kv = w_in_ref[:, K_OFF:K_OFF + 2 * KV_WIDTH]
    kv = jnp.concatenate([_dot(h_ref[:TILE // 2, :], w_kv), _dot(h_ref[TILE // 2:, :], w_kv)], axis=0)
    low = lax.broadcasted_iota(jnp.int32, (TILE, PAIR_WIDTH), 1) < HEAD_DIM

    def place(var_ref, t):
        zero = jnp.zeros_like(t)
        swapped = pltpu.roll(t, HEAD_DIM, 1)
        var_ref[0, WINDOW:, :] = jnp.where(low, t, zero)
        var_ref[1, WINDOW:, :] = jnp.where(low, zero, swapped)
        var_ref[2, WINDOW:, :] = jnp.where(low, swapped, zero)
        var_ref[3, WINDOW:, :] = jnp.where(low, zero, t)

    place(kvar_ref, kv[:, :KV_WIDTH].astype(bf16))
    place(vvar_ref, kv[:, KV_WIDTH:].astype(bf16))

    qi = lax.broadcasted_iota(jnp.int32, (WINDOW, 2 * WINDOW), 0)
    kj = lax.broadcasted_iota(jnp.int32, (WINDOW, 2 * WINDOW), 1)
    dist = qi + WINDOW - kj
    in_window = jnp.logical_or(jnp.logical_and(dist >= 0, dist < WINDOW), kj == 0)
    first_mask = jnp.logical_and(in_window, jnp.logical_or(jnp.logical_or(kj >= WINDOW, kj == 0), t_idx > 0))
    band_low = lax.broadcasted_iota(jnp.int32, (2 * WINDOW, PAIR_WIDTH), 1) < HEAD_DIM
    ones_even = jnp.where(band_low, 1.0, 0.0).astype(bf16)
    ones_odd = jnp.where(band_low, 0.0, 1.0).astype(bf16)
    first_row = lax.broadcasted_iota(jnp.int32, (BF16_SUBLANES, PAIR_WIDTH), 0) == 0

    def band(var_ref, variant, r0):
        head = var_ref[variant, r0:r0 + BF16_SUBLANES, :]
        head = jnp.where(first_row, jnp.zeros_like(head), head)
        return jnp.concatenate([head, var_ref[variant, r0 + BF16_SUBLANES:r0 + 2 * WINDOW, :]], axis=0)

    def attn_scores(blk, j):
        r0 = blk * WINDOW
        q_g = jnp.concatenate(
            [q_ref[r0:r0 + WINDOW, (PAIRS_PER_KV * j + p) * PAIR_WIDTH:(PAIRS_PER_KV * j + p + 1) * PAIR_WIDTH]
             for p in range(PAIRS_PER_KV)], axis=0)
        k_g = jnp.concatenate([band(kvar_ref, 2 * j + e, r0) for e in range(2)], axis=0)
        return lax.dot_general(q_g, k_g, (((1,), (1,)), ((), ())),
                               preferred_element_type=jnp.float32)

    def attn_softmax(blk, j, s_g):
        mask = first_mask if blk == 0 else in_window
        rows = []
        for p in range(PAIRS_PER_KV):
            probs = []
            for e in range(2):
                hq = GQA_GROUP * j + 2 * p + e
                s = s_g[p * WINDOW:(p + 1) * WINDOW, e * 2 * WINDOW:(e + 1) * 2 * WINDOW]
                s = jnp.where(mask, s + bias_ref[hq], NEG_INF)
                m = jnp.max(s, axis=-1, keepdims=True)
                probs.append(jnp.exp2(s - m).astype(bf16))
            rows.append(jnp.concatenate(probs, axis=1))
        return jnp.concatenate(rows, axis=0)

    def attn_pv(blk, j, p_g):
        r0 = blk * WINDOW
        v_g = jnp.concatenate(
            [jnp.concatenate([band(vvar_ref, 2 * j + e, r0), ones], axis=1)
             for e, ones in ((0, ones_even), (1, ones_odd))], axis=0)
        r_g = _dot(p_g, v_g)
        for p in range(PAIRS_PER_KV):
            r = r_g[p * WINDOW:(p + 1) * WINDOW]
            c0 = (PAIRS_PER_KV * j + p) * PAIR_WIDTH
            o_pair = r[:, :PAIR_WIDTH] / r[:, PAIR_WIDTH:]
            yain_ref[r0:r0 + WINDOW, c0:c0 + PAIR_WIDTH] = (
                o_pair * ga_ref[r0:r0 + WINDOW, c0:c0 + PAIR_WIDTH]).astype(bf16)

    def spatial_mix():
        n_chunks = TILE // CHUNK
        for g in range(SGU_GROUPS):
            c_lo = g * SGU_GROUP_DIM
            v_g = jnp.concatenate(
                [vn_ref[c * CHUNK:(c + 1) * CHUNK, c_lo:c_lo + SGU_GROUP_DIM] for c in range(n_chunks)], axis=1)
            m_g = _dot(wtril_ref[g], v_g) + b_sp_ref[g]
            for c in range(n_chunks):
                rows = slice(c * CHUNK, (c + 1) * CHUNK)
                cols = slice(c_lo, c_lo + SGU_GROUP_DIM)
                ybin_ref[rows, cols] = (m_g[:, c * SGU_GROUP_DIM:(c + 1) * SGU_GROUP_DIM]
                                        * t_ref[rows, cols]).astype(bf16)

    def job_v():
        v_act = _gelu_exact(proj(VB_OFF, SGU_WIDTH))
        mu = jnp.mean(v_act, axis=-1, keepdims=True)
        vc = v_act - mu
        vn_ref[...] = (vc * lax.rsqrt(jnp.mean(vc * vc, axis=-1, keepdims=True) + EPS) * lng_ref[...]
                       + lnb_ref[...]).astype(bf16)

    def job_u():
        gu_ref[...] = _gelu_exact(proj(U_OFF, SGU_WIDTH)).astype(bf16)

    def job_gate_b():
        t_ref[...] = (_silu(proj(GB_OFF, SGU_WIDTH)) * gu_ref[...]).astype(bf16)

    def job_merge_b():
        sb_ref[...] = _sigmoid(proj(MB_OFF, D_MODEL)).astype(bf16)

    def job_branch_b():
        tb_ref[...] = _dot(ybin_ref[...], w_b_ref[...]) * sb_ref[...]

    def job_merge_a():
        sa_ref[...] = _sigmoid(proj(MA_OFF, D_MODEL)).astype(bf16)

    dense_jobs = [job_v, job_u, job_gate_b, spatial_mix, job_merge_b, job_branch_b, job_merge_a]
    groups = [(blk, j) for blk in range(BLOCKS_PER_TILE) for j in range(N_KV_HEADS)]
    ga_ref[...] = _silu(proj(GA_OFF, ATTN_WIDTH)).astype(bf16)
    s_g = attn_scores(*groups[0])
    for gi, group in enumerate(groups):
        if dense_jobs:
            dense_jobs.pop(0)()
        p_g = attn_softmax(*group, s_g)
        if gi + 1 < len(groups):
            s_g = attn_scores(*groups[gi + 1])
        attn_pv(*group, p_g)

    for i in range(2 * N_KV_HEADS):
        kvar_ref[i, 0:WINDOW, :] = kvar_ref[i, TILE:TILE + WINDOW, :]
        vvar_ref[i, 0:WINDOW, :] = vvar_ref[i, TILE:TILE + WINDOW, :]

    for job in dense_jobs:
        job()
    merged_ref[...] = (_dot(yain_ref[...], w_a_ref[...]) * sa_ref[...] + tb_ref[...]).astype(bf16)


def _resident(shape):
    return pl.BlockSpec(shape, lambda s, *_: (0,) * len(shape), pipeline_mode=pl.Buffered(1))


def _layer_slice(layer, shape):
    return pl.BlockSpec((None,) + shape, lambda s, *_: (layer,) + (0,) * len(shape),
                        pipeline_mode=pl.Buffered(1))


def _layer(layer, x, rel_bias, sinks, bucket, pre, post, lng, lnb, w_in, w_sp, b_sp, w_a, w_b, w_o):
    B, S, D = x.shape
    assert D == D_MODEL and S % TILE == 0 and w_in.shape[1:] == (D_MODEL, IN_COLS)
    tiles_per_seq = S // TILE
    n_tiles = B * tiles_per_seq

    def tile_index(tile):
        return (tile // tiles_per_seq, tile % tiles_per_seq, 0)

    x_spec = pl.BlockSpec((1, TILE, D), lambda s, *_: tile_index(jnp.minimum(s, n_tiles - 1)))
    o_spec = pl.BlockSpec((1, TILE, D), lambda s, *_: tile_index(jnp.maximum(s - 1, 0)))
    stage_bf16 = pltpu.VMEM((TILE, D), jnp.bfloat16)
    stage_f32 = pltpu.VMEM((TILE, D), jnp.float32)
    grid_spec = pltpu.PrefetchScalarGridSpec(
        num_scalar_prefetch=2,
        grid=(n_tiles + 1,),
        in_specs=[
            x_spec,
            o_spec,
            _resident((WINDOW, 2 * WINDOW)),
            _layer_slice(layer, (1, D)), _layer_slice(layer, (1, D)),
            _layer_slice(layer, (1, SGU_WIDTH)), _layer_slice(layer, (1, SGU_WIDTH)),
            _layer_slice(layer, (D, IN_COLS)),
            _layer_slice(layer, (SGU_GROUPS, CHUNK, CHUNK)),
            _layer_slice(layer, (SGU_GROUPS, CHUNK, 1)),
            _layer_slice(layer, (ATTN_WIDTH, D)), _layer_slice(layer, (SGU_WIDTH, D)),
            _layer_slice(layer, (D, D)),
        ],
        out_specs=o_spec,
        scratch_shapes=[
            pltpu.VMEM((N_Q_HEADS, WINDOW, 2 * WINDOW), jnp.float32),
            pltpu.VMEM((SGU_GROUPS, CHUNK, CHUNK), jnp.bfloat16),
            pltpu.VMEM((2 * N_KV_HEADS, TILE + WINDOW, PAIR_WIDTH), jnp.bfloat16),
            pltpu.VMEM((2 * N_KV_HEADS, TILE + WINDOW, PAIR_WIDTH), jnp.bfloat16),
            stage_bf16,
            stage_bf16, stage_bf16,
            stage_bf16, stage_bf16,
            stage_bf16, stage_bf16, stage_bf16,
            stage_bf16, stage_bf16, stage_bf16,
            stage_f32,
        ],
    )
    return pl.pallas_call(
        functools.partial(_layer_kernel, tiles_per_seq=tiles_per_seq, layer=layer),
        out_shape=jax.ShapeDtypeStruct(x.shape, x.dtype),
        grid_spec=grid_spec,
        compiler_params=pltpu.CompilerParams(
            dimension_semantics=("arbitrary",),
            vmem_limit_bytes=V7X_VMEM_LIMIT_BYTES),
        name="hybrid_layer",
    )(rel_bias, sinks, x, x, bucket, pre, post, lng, lnb, w_in, w_sp, b_sp, w_a, w_b, w_o)


def kernel(x, rel_bias, pre_norm, post_norm, w_in, sinks, sgu_ln_g, sgu_ln_b, w_spatial, b_spatial,
           w_branch_a, w_branch_b, w_out):
    depth = w_in.shape[0]
    bucket = jnp.asarray(_t5_bucket_table())
    bf16 = jnp.bfloat16
    params = (rel_bias, sinks, bucket,
              pre_norm[:, None], post_norm[:, None], sgu_ln_g[:, None], sgu_ln_b[:, None],
              w_in.astype(bf16), w_spatial, b_spatial[..., None],
              w_branch_a.astype(bf16), w_branch_b.astype(bf16), w_out.astype(bf16))
    for l in range(depth):
        x = _layer(l, x, *params)
    return x
```

```python
import functools

import jax
import jax.numpy as jnp
import numpy as np
from jax import lax
from jax.experimental import pallas as pl
from jax.experimental.pallas import tpu as pltpu

D_MODEL = 1024
HEAD_DIM = 64
N_Q_HEADS = 16
N_KV_HEADS = 2
GQA_GROUP = N_Q_HEADS // N_KV_HEADS
WINDOW = 128
ATTN_WIDTH = N_Q_HEADS * HEAD_DIM
KV_WIDTH = N_KV_HEADS * HEAD_DIM
PAIR_WIDTH = 2 * HEAD_DIM
PAIRS_PER_KV = GQA_GROUP // 2
N_BUCKETS = 32
MAX_DISTANCE = 128
SGU_WIDTH = D_MODEL
SGU_GROUPS = 8
SGU_GROUP_DIM = SGU_WIDTH // SGU_GROUPS
CHUNK = 128
EPS = 1e-6
NEG_INF = -1e30
LOG2_E = float(np.log2(np.e))
BF16_SUBLANES = 16
CAST_ROWS = BF16_SUBLANES

_SPLITS = (ATTN_WIDTH, KV_WIDTH, KV_WIDTH, ATTN_WIDTH, SGU_WIDTH, SGU_WIDTH, SGU_WIDTH, D_MODEL, D_MODEL)
_OFFS = tuple(int(o) for o in np.cumsum((0,) + _SPLITS))
IN_COLS = _OFFS[-1]
(Q_OFF, K_OFF, V_OFF, GA_OFF, U_OFF, VB_OFF, GB_OFF, MA_OFF, MB_OFF) = _OFFS[:-1]

TILE = 256
BLOCKS_PER_TILE = TILE // WINDOW
V7X_VMEM_LIMIT_BYTES = 56 * 1024 * 1024


def _t5_bucket_table():
    qi = np.arange(WINDOW)[:, None]
    kj = np.arange(2 * WINDOW)[None, :]
    dist = np.clip(qi + WINDOW - kj, 0, None)
    max_exact = N_BUCKETS // 2
    large = max_exact + (np.log(np.maximum(dist, max_exact) / max_exact)
                         / np.log(MAX_DISTANCE / max_exact)
                         * (N_BUCKETS - max_exact)).astype(np.int32)
    large = np.minimum(large, N_BUCKETS - 1)
    return np.where(dist < max_exact, dist, large).astype(np.int32)


def _dot(a, b):
    return jnp.dot(a, b, preferred_element_type=jnp.float32)


def _gelu_exact(x):
    return 0.5 * x * (1.0 + lax.erf(x * np.float32(np.sqrt(0.5))))


def _sigmoid(x):
    return 0.5 * (1.0 + jnp.tanh(0.5 * x))


def _silu(x):
    half = 0.5 * x
    return half * (1.0 + jnp.tanh(half))


def _layer_kernel(rel_bias_ref, sinks_ref, *refs, tiles_per_seq, layer, n_cast):
    (x_ref, xprev_ref, bucket_ref, pre_ref, post_ref, lng_ref, lnb_ref,
     w_in_ref, w_sp_ref, b_sp_ref, w_a_ref, w_b_ref, w_o_ref) = refs[:13]
    cast_src = refs[13:13 + n_cast]
    o_ref = refs[13 + n_cast]
    cast_dst = refs[14 + n_cast:14 + 2 * n_cast]
    (bias_ref, wtril_ref, kvar_ref, vvar_ref, merged_ref,
     h_ref, q_ref, ga_ref, yain_ref, vn_ref, gu_ref, t_ref, ybin_ref, sb_ref, sa_ref, tb_ref) = refs[14 + 2 * n_cast:]
    step = pl.program_id(0)
    t_idx = jnp.minimum(step, pl.num_programs(0) - 2) % tiles_per_seq
    bf16 = jnp.bfloat16

    for src, dst in zip(cast_src, cast_dst):
        dst[...] = src[...].astype(bf16)

    @pl.when(step == 0)
    def _():
        merged_ref[...] = jnp.zeros_like(merged_ref)
        bucket = bucket_ref[...]
        sink_col = lax.broadcasted_iota(jnp.int32, (WINDOW, 2 * WINDOW), 1) == 0
        for h in range(N_Q_HEADS):
            acc = jnp.zeros((WINDOW, 2 * WINDOW), jnp.float32)
            for bk in range(N_BUCKETS):
                acc = jnp.where(bucket == bk, rel_bias_ref[bk, h], acc)
            bias_ref[h] = jnp.where(sink_col, sinks_ref[layer, h], acc) * LOG2_E
        row = lax.broadcasted_iota(jnp.int32, (CHUNK, CHUNK), 0)
        col = lax.broadcasted_iota(jnp.int32, (CHUNK, CHUNK), 1)
        for g in range(SGU_GROUPS):
            wtril_ref[g] = jnp.where(col <= row, w_sp_ref[g], 0.0).astype(bf16)

    @pl.when(t_idx == 0)
    def _():
        for i in range(2 * N_KV_HEADS):
            kvar_ref[i, 0:WINDOW, :] = jnp.zeros((WINDOW, PAIR_WIDTH), bf16)
            vvar_ref[i, 0:WINDOW, :] = jnp.zeros((WINDOW, PAIR_WIDTH), bf16)

    out = _dot(merged_ref[...], w_o_ref[...])
    o_ref[0] = (xprev_ref[0]
                + out * lax.rsqrt(jnp.mean(out * out, axis=-1, keepdims=True) + EPS) * post_ref[...])

    x = x_ref[0]
    h_ref[...] = (x * lax.rsqrt(jnp.mean(x * x, axis=-1, keepdims=True) + EPS) * pre_ref[...]).astype(bf16)

    def proj(off, width):
        return _dot(h_ref[...], w_in_ref[:, off:off + width])

    q_ref[...] = (proj(Q_OFF, ATTN_WIDTH) * np.float32(HEAD_DIM ** -0.5 * LOG2_E)).astype(bf16)
    w_kv = w_in_ref[:, K_OFF:K_OFF + 2 * KV_WIDTH]
    kv = jnp.concatenate([_dot(h_ref[:TILE // 2, :], w_kv), _dot(h_ref[TILE // 2:, :], w_kv)], axis=0)
    low = lax.broadcasted_iota(jnp.int32, (TILE, PAIR_WIDTH), 1) < HEAD_DIM

    def place(var_ref, t):
        zero = jnp.zeros_like(t)
        swapped = pltpu.roll(t, HEAD_DIM, 1)
        var_ref[0, WINDOW:, :] = jnp.where(low, t, zero)
        var_ref[1, WINDOW:, :] = jnp.where(low, zero, swapped)
        var_ref[2, WINDOW:, :] = jnp.where(low, swapped, zero)
        var_ref[3, WINDOW:, :] = jnp.where(low, zero, t)

    place(kvar_ref, kv[:, :KV_WIDTH].astype(bf16))
    place(vvar_ref, kv[:, KV_WIDTH:].astype(bf16))

    qi = lax.broadcasted_iota(jnp.int32, (WINDOW, 2 * WINDOW), 0)
    kj = lax.broadcasted_iota(jnp.int32, (WINDOW, 2 * WINDOW), 1)
    dist = qi + WINDOW - kj
    in_window = jnp.logical_or(jnp.logical_and(dist >= 0, dist < WINDOW), kj == 0)
    first_mask = jnp.logical_and(in_window, jnp.logical_or(jnp.logical_or(kj >= WINDOW, kj == 0), t_idx > 0))
    band_low = lax.broadcasted_iota(jnp.int32, (2 * WINDOW, PAIR_WIDTH), 1) < HEAD_DIM
    ones_even = jnp.where(band_low, 1.0, 0.0).astype(bf16)
    ones_odd = jnp.where(band_low, 0.0, 1.0).astype(bf16)
    first_row = lax.broadcasted_iota(jnp.int32, (BF16_SUBLANES, PAIR_WIDTH), 0) == 0

    def band(var_ref, variant, r0):
        head = var_ref[variant, r0:r0 + BF16_SUBLANES, :]
        head = jnp.where(first_row, jnp.zeros_like(head), head)
        return jnp.concatenate([head, var_ref[variant, r0 + BF16_SUBLANES:r0 + 2 * WINDOW, :]], axis=0)

    def attn_scores(blk, j):
        r0 = blk * WINDOW
        q_g = jnp.concatenate(
            [q_ref[r0:r0 + WINDOW, (PAIRS_PER_KV * j + p) * PAIR_WIDTH:(PAIRS_PER_KV * j + p + 1) * PAIR_WIDTH]
             for p in range(PAIRS_PER_KV)], axis=0)
        k_g = jnp.concatenate([band(kvar_ref, 2 * j + e, r0) for e in range(2)], axis=0)
        return lax.dot_general(q_g, k_g, (((1,), (1,)), ((), ())),
                               preferred_element_type=jnp.float32)

    def attn_softmax(blk, j, s_g):
        mask = first_mask if blk == 0 else in_window
        rows = []
        for p in range(PAIRS_PER_KV):
            probs = []
            for e in range(2):
                hq = GQA_GROUP * j + 2 * p + e
                s = s_g[p * WINDOW:(p + 1) * WINDOW, e * 2 * WINDOW:(e + 1) * 2 * WINDOW]
                s = jnp.where(mask, s + bias_ref[hq], NEG_INF)
                m = jnp.max(s, axis=-1, keepdims=True)
                probs.append(jnp.exp2(s - m).astype(bf16))
            rows.append(jnp.concatenate(probs, axis=1))
        return jnp.concatenate(rows, axis=0)

    def attn_pv(blk, j, p_g):
        r0 = blk * WINDOW
        v_g = jnp.concatenate(
            [jnp.concatenate([band(vvar_ref, 2 * j + e, r0), ones], axis=1)
             for e, ones in ((0, ones_even), (1, ones_odd))], axis=0)
        r_g = _dot(p_g, v_g)
        for p in range(PAIRS_PER_KV):
            r = r_g[p * WINDOW:(p + 1) * WINDOW]
            c0 = (PAIRS_PER_KV * j + p) * PAIR_WIDTH
            o_pair = r[:, :PAIR_WIDTH] / r[:, PAIR_WIDTH:]
            yain_ref[r0:r0 + WINDOW, c0:c0 + PAIR_WIDTH] = (
                o_pair * ga_ref[r0:r0 + WINDOW, c0:c0 + PAIR_WIDTH]).astype(bf16)

    def spatial_mix():
        n_chunks = TILE // CHUNK
        for g in range(SGU_GROUPS):
            c_lo = g * SGU_GROUP_DIM
            v_g = jnp.concatenate(
                [vn_ref[c * CHUNK:(c + 1) * CHUNK, c_lo:c_lo + SGU_GROUP_DIM] for c in range(n_chunks)], axis=1)
            m_g = _dot(wtril_ref[g], v_g) + b_sp_ref[g]
            for c in range(n_chunks):
                rows = slice(c * CHUNK, (c + 1) * CHUNK)
                cols = slice(c_lo, c_lo + SGU_GROUP_DIM)
                ybin_ref[rows, cols] = (m_g[:, c * SGU_GROUP_DIM:(c + 1) * SGU_GROUP_DIM]
                                        * t_ref[rows, cols]).astype(bf16)

    def job_v():
        v_act = _gelu_exact(proj(VB_OFF, SGU_WIDTH))
        mu = jnp.mean(v_act, axis=-1, keepdims=True)
        vc = v_act - mu
        vn_ref[...] = (vc * lax.rsqrt(jnp.mean(vc * vc, axis=-1, keepdims=True) + EPS) * lng_ref[...]
                       + lnb_ref[...]).astype(bf16)

    def job_u():
        gu_ref[...] = _gelu_exact(proj(U_OFF, SGU_WIDTH)).astype(bf16)

    def job_gate_b():
        t_ref[...] = (_silu(proj(GB_OFF, SGU_WIDTH)) * gu_ref[...]).astype(bf16)

    def job_merge_b():
        sb_ref[...] = _sigmoid(proj(MB_OFF, D_MODEL)).astype(bf16)

    def job_branch_b():
        tb_ref[...] = _dot(ybin_ref[...], w_b_ref[...]) * sb_ref[...]

    def job_merge_a():
        sa_ref[...] = _sigmoid(proj(MA_OFF, D_MODEL)).astype(bf16)

    dense_jobs = [job_v, job_u, job_gate_b, spatial_mix, job_merge_b, job_branch_b, job_merge_a]
    groups = [(blk, j) for blk in range(BLOCKS_PER_TILE) for j in range(N_KV_HEADS)]
    ga_ref[...] = _silu(proj(GA_OFF, ATTN_WIDTH)).astype(bf16)
    s_g = attn_scores(*groups[0])
    for gi, group in enumerate(groups):
        if dense_jobs:
            dense_jobs.pop(0)()
        p_g = attn_softmax(*group, s_g)
        if gi + 1 < len(groups):
            s_g = attn_scores(*groups[gi + 1])
        attn_pv(*group, p_g)

    for i in range(2 * N_KV_HEADS):
        kvar_ref[i, 0:WINDOW, :] = kvar_ref[i, TILE:TILE + WINDOW, :]
        vvar_ref[i, 0:WINDOW, :] = vvar_ref[i, TILE:TILE + WINDOW, :]

    for job in dense_jobs:
        job()
    merged_ref[...] = (_dot(yain_ref[...], w_a_ref[...]) * sa_ref[...] + tb_ref[...]).astype(bf16)


def _resident(shape):
    return pl.BlockSpec(shape, lambda s, *_: (0,) * len(shape), pipeline_mode=pl.Buffered(1))


def _layer_slice(layer, shape):
    return pl.BlockSpec((None,) + shape, lambda s, *_: (layer,) + (0,) * len(shape),
                        pipeline_mode=pl.Buffered(1))


def _layer(layer, x, rel_bias, sinks, bucket, pre, post, lng, lnb, w_sp, b_sp, weights, next_weights):
    B, S, D = x.shape
    w_in, w_a, w_b, w_o = weights
    assert D == D_MODEL and S % TILE == 0 and w_in.shape == (1, D_MODEL, IN_COLS)
    tiles_per_seq = S // TILE
    n_tiles = B * tiles_per_seq
    cast_blocks = D_MODEL // CAST_ROWS
    assert n_tiles >= 2 * cast_blocks or not next_weights

    def tile_index(tile):
        return (tile // tiles_per_seq, tile % tiles_per_seq, 0)

    def cast_block(s):
        return jnp.minimum(s // (n_tiles // cast_blocks), cast_blocks - 1)

    x_spec = pl.BlockSpec((1, TILE, D), lambda s, *_: tile_index(jnp.minimum(s, n_tiles - 1)))
    o_spec = pl.BlockSpec((1, TILE, D), lambda s, *_: tile_index(jnp.maximum(s - 1, 0)))
    cast_in_specs = [pl.BlockSpec((None, CAST_ROWS, w.shape[2]), lambda s, *_: (layer + 1, cast_block(s), 0))
                     for w in next_weights]
    cast_out_specs = [pl.BlockSpec((None, CAST_ROWS, w.shape[2]), lambda s, *_: (0, cast_block(s), 0))
                      for w in next_weights]
    cast_out_shapes = [jax.ShapeDtypeStruct((1,) + w.shape[1:], jnp.bfloat16) for w in next_weights]
    stage_bf16 = pltpu.VMEM((TILE, D), jnp.bfloat16)
    stage_f32 = pltpu.VMEM((TILE, D), jnp.float32)
    grid_spec = pltpu.PrefetchScalarGridSpec(
        num_scalar_prefetch=2,
        grid=(n_tiles + 1,),
        in_specs=[
            x_spec,
            o_spec,
            _resident((WINDOW, 2 * WINDOW)),
            _layer_slice(layer, (1, D)), _layer_slice(layer, (1, D)),
            _layer_slice(layer, (1, SGU_WIDTH)), _layer_slice(layer, (1, SGU_WIDTH)),
            _layer_slice(0, (D, IN_COLS)),
            _layer_slice(layer, (SGU_GROUPS, CHUNK, CHUNK)),
            _layer_slice(layer, (SGU_GROUPS, CHUNK, 1)),
            _layer_slice(0, (ATTN_WIDTH, D)), _layer_slice(0, (SGU_WIDTH, D)),
            _layer_slice(0, (D, D)),
        ] + cast_in_specs,
        out_specs=[o_spec] + cast_out_specs,
        scratch_shapes=[
            pltpu.VMEM((N_Q_HEADS, WINDOW, 2 * WINDOW), jnp.float32),
            pltpu.VMEM((SGU_GROUPS, CHUNK, CHUNK), jnp.bfloat16),
            pltpu.VMEM((2 * N_KV_HEADS, TILE + WINDOW, PAIR_WIDTH), jnp.bfloat16),
            pltpu.VMEM((2 * N_KV_HEADS, TILE + WINDOW, PAIR_WIDTH), jnp.bfloat16),
            stage_bf16,
            stage_bf16, stage_bf16,
            stage_bf16, stage_bf16,
            stage_bf16, stage_bf16, stage_bf16,
            stage_bf16, stage_bf16, stage_bf16,
            stage_f32,
        ],
    )
    outs = pl.pallas_call(
        functools.partial(_layer_kernel, tiles_per_seq=tiles_per_seq, layer=layer, n_cast=len(next_weights)),
        out_shape=[jax.ShapeDtypeStruct(x.shape, x.dtype)] + cast_out_shapes,
        grid_spec=grid_spec,
        compiler_params=pltpu.CompilerParams(
            dimension_semantics=("arbitrary",),
            vmem_limit_bytes=V7X_VMEM_LIMIT_BYTES),
        name="hybrid_layer",
    )(rel_bias, sinks, x, x, bucket, pre, post, lng, lnb, w_in, w_sp, b_sp, w_a, w_b, w_o, *next_weights)
    return outs[0], tuple(outs[1:])


def kernel(x, rel_bias, pre_norm, post_norm, w_in, sinks, sgu_ln_g, sgu_ln_b, w_spatial, b_spatial,
           w_branch_a, w_branch_b, w_out):
    depth = w_in.shape[0]
    bucket = jnp.asarray(_t5_bucket_table())
    small = (rel_bias, sinks, bucket,
             pre_norm[:, None], post_norm[:, None], sgu_ln_g[:, None], sgu_ln_b[:, None],
             w_spatial, b_spatial[..., None])
    stacked = (w_in, w_branch_a, w_branch_b, w_out)
    weights = tuple(w[:1].astype(jnp.bfloat16) for w in stacked)
    for l in range(depth):
        x, weights = _layer(l, x, *small, weights, stacked if l + 1 < depth else ())
    return x
```

```python
import functools

import jax
import jax.numpy as jnp
import numpy as np
from jax import lax
from jax.experimental import pallas as pl
from jax.experimental.pallas import tpu as pltpu

D_MODEL = 1024
HEAD_DIM = 64
N_Q_HEADS = 16
N_KV_HEADS = 2
GQA_GROUP = N_Q_HEADS // N_KV_HEADS
WINDOW = 128
ATTN_WIDTH = N_Q_HEADS * HEAD_DIM
KV_WIDTH = N_KV_HEADS * HEAD_DIM
PAIR_WIDTH = 2 * HEAD_DIM
PAIRS_PER_KV = GQA_GROUP // 2
N_BUCKETS = 32
MAX_DISTANCE = 128
SGU_WIDTH = D_MODEL
SGU_GROUPS = 8
SGU_GROUP_DIM = SGU_WIDTH // SGU_GROUPS
CHUNK = 128
EPS = 1e-6
NEG_INF = -1e30
LOG2_E = float(np.log2(np.e))
BF16_SUBLANES = 16
CAST_ROWS = BF16_SUBLANES

_SPLITS = (ATTN_WIDTH, KV_WIDTH, KV_WIDTH, ATTN_WIDTH, SGU_WIDTH, SGU_WIDTH, SGU_WIDTH, D_MODEL, D_MODEL)
_OFFS = tuple(int(o) for o in np.cumsum((0,) + _SPLITS))
IN_COLS = _OFFS[-1]
(Q_OFF, K_OFF, V_OFF, GA_OFF, U_OFF, VB_OFF, GB_OFF, MA_OFF, MB_OFF) = _OFFS[:-1]

TILE = 256
BLOCKS_PER_TILE = TILE // WINDOW
V7X_VMEM_LIMIT_BYTES = 56 * 1024 * 1024


def _t5_bucket_row():
    kj = np.arange(2 * WINDOW)[None, :]
    dist = np.broadcast_to(np.clip(WINDOW - kj, 0, WINDOW - 1), (8, 2 * WINDOW))
    max_exact = N_BUCKETS // 2
    large = max_exact + (np.log(np.maximum(dist, max_exact) / max_exact)
                         / np.log(MAX_DISTANCE / max_exact)
                         * (N_BUCKETS - max_exact)).astype(np.int32)
    large = np.minimum(large, N_BUCKETS - 1)
    return np.where(dist < max_exact, dist, large).astype(np.int32)


def _dot(a, b):
    return jnp.dot(a, b, preferred_element_type=jnp.float32)


def _gelu_exact(x):
    return 0.5 * x * (1.0 + lax.erf(x * np.float32(np.sqrt(0.5))))


def _sigmoid(x):
    return 0.5 * (1.0 + jnp.tanh(0.5 * x))


def _silu(x):
    half = 0.5 * x
    return half * (1.0 + jnp.tanh(half))


def _layer_kernel(rel_bias_ref, sinks_ref, *refs, tiles_per_seq, layer, n_cast):
    (x_ref, xprev_ref, bucket_ref, pre_ref, post_ref, lng_ref, lnb_ref,
     w_in_ref, w_sp_ref, b_sp_ref, w_a_ref, w_b_ref, w_o_ref) = refs[:13]
    cast_src = refs[13:13 + n_cast]
    o_ref = refs[13 + n_cast]
    cast_dst = refs[14 + n_cast:14 + 2 * n_cast]
    (bias_ref, wtril_ref, kvar_ref, vvar_ref, merged_ref,
     h_ref, q_ref, ga_ref, yain_ref, vn_ref, gu_ref, t_ref, ybin_ref, sb_ref, sa_ref, tb_ref) = refs[14 + 2 * n_cast:]
    step = pl.program_id(0)
    t_idx = jnp.minimum(step, pl.num_programs(0) - 2) % tiles_per_seq
    bf16 = jnp.bfloat16

    for src, dst in zip(cast_src, cast_dst):
        dst[...] = src[...].astype(bf16)

    @pl.when(step == 0)
    def _():
        merged_ref[...] = jnp.zeros_like(merged_ref)
        bucket = bucket_ref[...]
        sink_col = lax.broadcasted_iota(jnp.int32, (WINDOW, 2 * WINDOW), 1) == 0
        for h in range(N_Q_HEADS):
            row0 = jnp.zeros(bucket.shape, jnp.float32)
            for bk in range(N_BUCKETS):
                row0 = jnp.where(bucket == bk, rel_bias_ref[bk, h], row0)
            band_bias = pltpu.roll(jnp.concatenate([row0] * (WINDOW // 8), axis=0), 0, 1, stride=1, stride_axis=0)
            bias_ref[h] = jnp.where(sink_col, sinks_ref[layer, h], band_bias) * LOG2_E
        row = lax.broadcasted_iota(jnp.int32, (CHUNK, CHUNK), 0)
        col = lax.broadcasted_iota(jnp.int32, (CHUNK, CHUNK), 1)
        for g in range(SGU_GROUPS):
            wtril_ref[g] = jnp.where(col <= row, w_sp_ref[g], 0.0).astype(bf16)

    @pl.when(t_idx == 0)
    def _():
        for i in range(2 * N_KV_HEADS):
            kvar_ref[i, 0:WINDOW, :] = jnp.zeros((WINDOW, PAIR_WIDTH), bf16)
            vvar_ref[i, 0:WINDOW, :] = jnp.zeros((WINDOW, PAIR_WIDTH), bf16)

    out = _dot(merged_ref[...], w_o_ref[...])
    o_ref[0] = (xprev_ref[0]
                + out * lax.rsqrt(jnp.mean(out * out, axis=-1, keepdims=True) + EPS) * post_ref[...])

    x = x_ref[0]
    h_ref[...] = (x * lax.rsqrt(jnp.mean(x * x, axis=-1, keepdims=True) + EPS) * pre_ref[...]).astype(bf16)

    def proj(off, width):
        return _dot(h_ref[...], w_in_ref[:, off:off + width])

    q_ref[...] = (proj(Q_OFF, ATTN_WIDTH) * np.float32(HEAD_DIM ** -0.5 * LOG2_E)).astype(bf16)
    w_kv = w_in_ref[:, K_OFF:K_OFF + 2 * KV_WIDTH]
    kv = jnp.concatenate([_dot(h_ref[:TILE // 2, :], w_kv), _dot(h_ref[TILE // 2:, :], w_kv)], axis=0)
    low = lax.broadcasted_iota(jnp.int32, (TILE, PAIR_WIDTH), 1) < HEAD_DIM

    def place(var_ref, t):
        zero = jnp.zeros_like(t)
        swapped = pltpu.roll(t, HEAD_DIM, 1)
        var_ref[0, WINDOW:, :] = jnp.where(low, t, zero)
        var_ref[1, WINDOW:, :] = jnp.where(low, zero, swapped)
        var_ref[2, WINDOW:, :] = jnp.where(low, swapped, zero)
        var_ref[3, WINDOW:, :] = jnp.where(low, zero, t)

    place(kvar_ref, kv[:, :KV_WIDTH].astype(bf16))
    place(vvar_ref, kv[:, KV_WIDTH:].astype(bf16))

    qi = lax.broadcasted_iota(jnp.int32, (WINDOW, 2 * WINDOW), 0)
    kj = lax.broadcasted_iota(jnp.int32, (WINDOW, 2 * WINDOW), 1)
    dist = qi + WINDOW - kj
    in_window = jnp.logical_or(jnp.logical_and(dist >= 0, dist < WINDOW), kj == 0)
    first_mask = jnp.logical_and(in_window, jnp.logical_or(jnp.logical_or(kj >= WINDOW, kj == 0), t_idx > 0))
    band_low = lax.broadcasted_iota(jnp.int32, (2 * WINDOW, PAIR_WIDTH), 1) < HEAD_DIM
    ones_even = jnp.where(band_low, 1.0, 0.0).astype(bf16)
    ones_odd = jnp.where(band_low, 0.0, 1.0).astype(bf16)
    first_row = lax.broadcasted_iota(jnp.int32, (BF16_SUBLANES, PAIR_WIDTH), 0) == 0

    def band(var_ref, variant, r0):
        head = var_ref[variant, r0:r0 + BF16_SUBLANES, :]
        head = jnp.where(first_row, jnp.zeros_like(head), head)
        return jnp.concatenate([head, var_ref[variant, r0 + BF16_SUBLANES:r0 + 2 * WINDOW, :]], axis=0)

    def attn_scores(blk, j):
        r0 = blk * WINDOW
        q_g = jnp.concatenate(
            [q_ref[r0:r0 + WINDOW, (PAIRS_PER_KV * j + p) * PAIR_WIDTH:(PAIRS_PER_KV * j + p + 1) * PAIR_WIDTH]
             for p in range(PAIRS_PER_KV)], axis=0)
        k_g = jnp.concatenate([band(kvar_ref, 2 * j + e, r0) for e in range(2)], axis=0)
        return lax.dot_general(q_g, k_g, (((1,), (1,)), ((), ())),
                               preferred_element_type=jnp.float32)

    def attn_softmax(blk, j, s_g):
        mask = first_mask if blk == 0 else in_window
        rows = []
        for p in range(PAIRS_PER_KV):
            probs = []
            for e in range(2):
                hq = GQA_GROUP * j + 2 * p + e
                s = s_g[p * WINDOW:(p + 1) * WINDOW, e * 2 * WINDOW:(e + 1) * 2 * WINDOW]
                s = jnp.where(mask, s + bias_ref[hq], NEG_INF)
                m = jnp.max(s, axis=-1, keepdims=True)
                probs.append(jnp.exp2(s - m).astype(bf16))
            rows.append(jnp.concatenate(probs, axis=1))
        return jnp.concatenate(rows, axis=0)

    def attn_pv(blk, j, p_g):
        r0 = blk * WINDOW
        v_g = jnp.concatenate(
            [jnp.concatenate([band(vvar_ref, 2 * j + e, r0), ones], axis=1)
             for e, ones in ((0, ones_even), (1, ones_odd))], axis=0)
        r_g = _dot(p_g, v_g)
        for p in range(PAIRS_PER_KV):
            r = r_g[p * WINDOW:(p + 1) * WINDOW]
            c0 = (PAIRS_PER_KV * j + p) * PAIR_WIDTH
            o_pair = r[:, :PAIR_WIDTH] / r[:, PAIR_WIDTH:]
            yain_ref[r0:r0 + WINDOW, c0:c0 + PAIR_WIDTH] = (
                o_pair * ga_ref[r0:r0 + WINDOW, c0:c0 + PAIR_WIDTH]).astype(bf16)

    def spatial_mix():
        n_chunks = TILE // CHUNK
        for g in range(SGU_GROUPS):
            c_lo = g * SGU_GROUP_DIM
            v_g = jnp.concatenate(
                [vn_ref[c * CHUNK:(c + 1) * CHUNK, c_lo:c_lo + SGU_GROUP_DIM] for c in range(n_chunks)], axis=1)
            m_g = _dot(wtril_ref[g], v_g) + b_sp_ref[g]
            for c in range(n_chunks):
                rows = slice(c * CHUNK, (c + 1) * CHUNK)
                cols = slice(c_lo, c_lo + SGU_GROUP_DIM)
                ybin_ref[rows, cols] = (m_g[:, c * SGU_GROUP_DIM:(c + 1) * SGU_GROUP_DIM]
                                        * t_ref[rows, cols]).astype(bf16)

    def job_v():
        v_act = _gelu_exact(proj(VB_OFF, SGU_WIDTH))
        mu = jnp.mean(v_act, axis=-1, keepdims=True)
        vc = v_act - mu
        vn_ref[...] = (vc * lax.rsqrt(jnp.mean(vc * vc, axis=-1, keepdims=True) + EPS) * lng_ref[...]
                       + lnb_ref[...]).astype(bf16)

    def job_u():
        gu_ref[...] = _gelu_exact(proj(U_OFF, SGU_WIDTH)).astype(bf16)

    def job_gate_b():
        t_ref[...] = (_silu(proj(GB_OFF, SGU_WIDTH)) * gu_ref[...]).astype(bf16)

    def job_merge_b():
        sb_ref[...] = _sigmoid(proj(MB_OFF, D_MODEL)).astype(bf16)

    def job_branch_b():
        tb_ref[...] = _dot(ybin_ref[...], w_b_ref[...]) * sb_ref[...]

    def job_merge_a():
        sa_ref[...] = _sigmoid(proj(MA_OFF, D_MODEL)).astype(bf16)

    dense_jobs = [job_v, job_u, job_gate_b, spatial_mix, job_merge_b, job_branch_b, job_merge_a]
    groups = [(blk, j) for blk in range(BLOCKS_PER_TILE) for j in range(N_KV_HEADS)]
    ga_ref[...] = _silu(proj(GA_OFF, ATTN_WIDTH)).astype(bf16)
    s_g = attn_scores(*groups[0])
    for gi, group in enumerate(groups):
        if dense_jobs:
            dense_jobs.pop(0)()
        p_g = attn_softmax(*group, s_g)
        if gi + 1 < len(groups):
            s_g = attn_scores(*groups[gi + 1])
        attn_pv(*group, p_g)

    for i in range(2 * N_KV_HEADS):
        kvar_ref[i, 0:WINDOW, :] = kvar_ref[i, TILE:TILE + WINDOW, :]
        vvar_ref[i, 0:WINDOW, :] = vvar_ref[i, TILE:TILE + WINDOW, :]

    for job in dense_jobs:
        job()
    merged_ref[...] = (_dot(yain_ref[...], w_a_ref[...]) * sa_ref[...] + tb_ref[...]).astype(bf16)


def _resident(shape):
    return pl.BlockSpec(shape, lambda s, *_: (0,) * len(shape), pipeline_mode=pl.Buffered(1))


def _layer_slice(layer, shape):
    return pl.BlockSpec((None,) + shape, lambda s, *_: (layer,) + (0,) * len(shape),
                        pipeline_mode=pl.Buffered(1))


def _layer(layer, x, rel_bias, sinks, bucket, pre, post, lng, lnb, w_sp, b_sp, weights, next_weights):
    B, S, D = x.shape
    w_in, w_a, w_b, w_o = weights
    assert D == D_MODEL and S % TILE == 0 and w_in.shape == (1, D_MODEL, IN_COLS)
    tiles_per_seq = S // TILE
    n_tiles = B * tiles_per_seq
    cast_blocks = D_MODEL // CAST_ROWS
    assert n_tiles >= 2 * cast_blocks or not next_weights

    def tile_index(tile):
        return (tile // tiles_per_seq, tile % tiles_per_seq, 0)

    def cast_block(s):
        return jnp.minimum(s // (n_tiles // cast_blocks), cast_blocks - 1)

    x_spec = pl.BlockSpec((1, TILE, D), lambda s, *_: tile_index(jnp.minimum(s, n_tiles - 1)))
    o_spec = pl.BlockSpec((1, TILE, D), lambda s, *_: tile_index(jnp.maximum(s - 1, 0)))
    cast_in_specs = [pl.BlockSpec((None, CAST_ROWS, w.shape[2]), lambda s, *_: (layer + 1, cast_block(s), 0))
                     for w in next_weights]
    cast_out_specs = [pl.BlockSpec((None, CAST_ROWS, w.shape[2]), lambda s, *_: (0, cast_block(s), 0))
                      for w in next_weights]
    cast_out_shapes = [jax.ShapeDtypeStruct((1,) + w.shape[1:], jnp.bfloat16) for w in next_weights]
    stage_bf16 = pltpu.VMEM((TILE, D), jnp.bfloat16)
    stage_f32 = pltpu.VMEM((TILE, D), jnp.float32)
    grid_spec = pltpu.PrefetchScalarGridSpec(
        num_scalar_prefetch=2,
        grid=(n_tiles + 1,),
        in_specs=[
            x_spec,
            o_spec,
            _resident((8, 2 * WINDOW)),
            _layer_slice(layer, (1, D)), _layer_slice(layer, (1, D)),
            _layer_slice(layer, (1, SGU_WIDTH)), _layer_slice(layer, (1, SGU_WIDTH)),
            _layer_slice(0, (D, IN_COLS)),
            _layer_slice(layer, (SGU_GROUPS, CHUNK, CHUNK)),
            _layer_slice(layer, (SGU_GROUPS, CHUNK, 1)),
            _layer_slice(0, (ATTN_WIDTH, D)), _layer_slice(0, (SGU_WIDTH, D)),
            _layer_slice(0, (D, D)),
        ] + cast_in_specs,
        out_specs=[o_spec] + cast_out_specs,
        scratch_shapes=[
            pltpu.VMEM((N_Q_HEADS, WINDOW, 2 * WINDOW), jnp.float32),
            pltpu.VMEM((SGU_GROUPS, CHUNK, CHUNK), jnp.bfloat16),
            pltpu.VMEM((2 * N_KV_HEADS, TILE + WINDOW, PAIR_WIDTH), jnp.bfloat16),
            pltpu.VMEM((2 * N_KV_HEADS, TILE + WINDOW, PAIR_WIDTH), jnp.bfloat16),
            stage_bf16,
            stage_bf16, stage_bf16,
            stage_bf16, stage_bf16,
            stage_bf16, stage_bf16, stage_bf16,
            stage_bf16, stage_bf16, stage_bf16,
            stage_f32,
        ],
    )
    outs = pl.pallas_call(
        functools.partial(_layer_kernel, tiles_per_seq=tiles_per_seq, layer=layer, n_cast=len(next_weights)),
        out_shape=[jax.ShapeDtypeStruct(x.shape, x.dtype)] + cast_out_shapes,
        grid_spec=grid_spec,
        compiler_params=pltpu.CompilerParams(
            dimension_semantics=("arbitrary",),
            vmem_limit_bytes=V7X_VMEM_LIMIT_BYTES),
        name="hybrid_layer",
    )(rel_bias, sinks, x, x, bucket, pre, post, lng, lnb, w_in, w_sp, b_sp, w_a, w_b, w_o, *next_weights)
    return outs[0], tuple(outs[1:])


def kernel(x, rel_bias, pre_norm, post_norm, w_in, sinks, sgu_ln_g, sgu_ln_b, w_spatial, b_spatial,
           w_branch_a, w_branch_b, w_out):
    depth = w_in.shape[0]
    bucket = jnp.asarray(_t5_bucket_row())
    small = (rel_bias, sinks, bucket,
             pre_norm[:, None], post_norm[:, None], sgu_ln_g[:, None], sgu_ln_b[:, None],
             w_spatial, b_spatial[..., None])
    stacked = (w_in, w_branch_a, w_branch_b, w_out)
    weights = tuple(w[:1].astype(jnp.bfloat16) for w in stacked)
    for l in range(depth):
        x, weights = _layer(l, x, *small, weights, stacked if l + 1 < depth else ())
    return x
```

```python
import functools

import jax
import jax.numpy as jnp
import numpy as np
from jax import lax
from jax.experimental import pallas as pl
from jax.experimental.pallas import tpu as pltpu

D_MODEL = 1024
HEAD_DIM = 64
N_Q_HEADS = 16
N_KV_HEADS = 2
GQA_GROUP = N_Q_HEADS // N_KV_HEADS
WINDOW = 128
ATTN_WIDTH = N_Q_HEADS * HEAD_DIM
KV_WIDTH = N_KV_HEADS * HEAD_DIM
PAIR_WIDTH = 2 * HEAD_DIM
PAIRS_PER_KV = GQA_GROUP // 2
N_BUCKETS = 32
MAX_DISTANCE = 128
SGU_WIDTH = D_MODEL
SGU_GROUPS = 8
SGU_GROUP_DIM = SGU_WIDTH // SGU_GROUPS
CHUNK = 128
EPS = 1e-6
NEG_INF = -1e30
LOG2_E = float(np.log2(np.e))
BF16_SUBLANES = 16
CAST_ROWS = BF16_SUBLANES

_SPLITS = (ATTN_WIDTH, KV_WIDTH, KV_WIDTH, ATTN_WIDTH, SGU_WIDTH, SGU_WIDTH, SGU_WIDTH, D_MODEL, D_MODEL)
_OFFS = tuple(int(o) for o in np.cumsum((0,) + _SPLITS))
IN_COLS = _OFFS[-1]
(Q_OFF, K_OFF, V_OFF, GA_OFF, U_OFF, VB_OFF, GB_OFF, MA_OFF, MB_OFF) = _OFFS[:-1]

TILE = 256
BLOCKS_PER_TILE = TILE // WINDOW
V7X_VMEM_LIMIT_BYTES = 56 * 1024 * 1024


def _t5_bucket_table():
    qi = np.arange(WINDOW)[:, None]
    kj = np.arange(2 * WINDOW)[None, :]
    dist = np.clip(qi + WINDOW - kj, 0, None)
    max_exact = N_BUCKETS // 2
    large = max_exact + (np.log(np.maximum(dist, max_exact) / max_exact)
                         / np.log(MAX_DISTANCE / max_exact)
                         * (N_BUCKETS - max_exact)).astype(np.int32)
    large = np.minimum(large, N_BUCKETS - 1)
    return np.where(dist < max_exact, dist, large).astype(np.int32)


def _dot(a, b):
    return jnp.dot(a, b, preferred_element_type=jnp.float32)


def _gelu_exact(x):
    return 0.5 * x * (1.0 + lax.erf(x * np.float32(np.sqrt(0.5))))


def _sigmoid(x):
    return 0.5 * (1.0 + jnp.tanh(0.5 * x))


def _silu(x):
    half = 0.5 * x
    return half * (1.0 + jnp.tanh(half))


def _layer_kernel(rel_bias_ref, sinks_ref, *refs, tiles_per_seq, layer, n_cast):
    (x_ref, xprev_ref, bucket_ref, pre_ref, post_ref, lng_ref, lnb_ref,
     w_in_ref, w_sp_ref, b_sp_ref, w_a_ref, w_b_ref, w_o_ref) = refs[:13]
    cast_src = refs[13:13 + n_cast]
    o_ref = refs[13 + n_cast]
    cast_dst = refs[14 + n_cast:14 + 2 * n_cast]
    (bias_ref, wtril_ref, kvar_ref, vvar_ref, merged_ref,
     h_ref, q_ref, ga_ref, yain_ref, vn_ref, gu_ref, t_ref, ybin_ref, sb_ref, sa_ref, tb_ref) = refs[14 + 2 * n_cast:]
    step = pl.program_id(0)
    t_idx = jnp.minimum(step, pl.num_programs(0) - 2) % tiles_per_seq
    bf16 = jnp.bfloat16

    for src, dst in zip(cast_src, cast_dst):
        dst[...] = src[...].astype(bf16)

    @pl.when(step == 0)
    def _():
        merged_ref[...] = jnp.zeros_like(merged_ref)
        bucket = bucket_ref[...]
        sink_col = lax.broadcasted_iota(jnp.int32, (WINDOW, 2 * WINDOW), 1) == 0
        for h in range(N_Q_HEADS):
            acc = jnp.zeros((WINDOW, 2 * WINDOW), jnp.float32)
            for bk in range(N_BUCKETS):
                acc = jnp.where(bucket == bk, rel_bias_ref[bk, h], acc)
            bias_ref[h] = jnp.where(sink_col, sinks_ref[layer, h], acc) * LOG2_E
        row = lax.broadcasted_iota(jnp.int32, (CHUNK, CHUNK), 0)
        col = lax.broadcasted_iota(jnp.int32, (CHUNK, CHUNK), 1)
        for g in range(SGU_GROUPS):
            wtril_ref[g] = jnp.where(col <= row, w_sp_ref[g], 0.0).astype(bf16)

    @pl.when(t_idx == 0)
    def _():
        for i in range(2 * N_KV_HEADS):
            kvar_ref[i, 0:WINDOW, :] = jnp.zeros((WINDOW, PAIR_WIDTH), bf16)
            vvar_ref[i, 0:WINDOW, :] = jnp.zeros((WINDOW, PAIR_WIDTH), bf16)

    out = _dot(merged_ref[...], w_o_ref[...])
    o_ref[0] = (xprev_ref[0]
                + out * lax.rsqrt(jnp.mean(out * out, axis=-1, keepdims=True) + EPS) * post_ref[...])

    x = x_ref[0]
    h_ref[...] = (x * lax.rsqrt(jnp.mean(x * x, axis=-1, keepdims=True) + EPS) * pre_ref[...]).astype(bf16)

    def proj(off, width):
        return _dot(h_ref[...], w_in_ref[:, off:off + width])

    q_ref[...] = (proj(Q_OFF, ATTN_WIDTH) * np.float32(HEAD_DIM ** -0.5 * LOG2_E)).astype(bf16)
    w_kv = w_in_ref[:, K_OFF:K_OFF + 2 * KV_WIDTH]
    kv = jnp.concatenate([_dot(h_ref[:TILE // 2, :], w_kv), _dot(h_ref[TILE // 2:, :], w_kv)], axis=0)
    low = lax.broadcasted_iota(jnp.int32, (TILE, PAIR_WIDTH), 1) < HEAD_DIM

    def place(var_ref, t):
        zero = jnp.zeros_like(t)
        swapped = pltpu.roll(t, HEAD_DIM, 1)
        var_ref[0, WINDOW:, :] = jnp.where(low, t, zero)
        var_ref[1, WINDOW:, :] = jnp.where(low, zero, swapped)
        var_ref[2, WINDOW:, :] = jnp.where(low, swapped, zero)
        var_ref[3, WINDOW:, :] = jnp.where(low, zero, t)

    place(kvar_ref, kv[:, :KV_WIDTH].astype(bf16))
    place(vvar_ref, kv[:, KV_WIDTH:].astype(bf16))

    qi = lax.broadcasted_iota(jnp.int32, (WINDOW, 2 * WINDOW), 0)
    kj = lax.broadcasted_iota(jnp.int32, (WINDOW, 2 * WINDOW), 1)
    dist = qi + WINDOW - kj
    in_window = jnp.logical_or(jnp.logical_and(dist >= 0, dist < WINDOW), kj == 0)
    first_mask = jnp.logical_and(in_window, jnp.logical_or(jnp.logical_or(kj >= WINDOW, kj == 0), t_idx > 0))
    band_low = lax.broadcasted_iota(jnp.int32, (2 * WINDOW, PAIR_WIDTH), 1) < HEAD_DIM
    ones_even = jnp.where(band_low, 1.0, 0.0).astype(bf16)
    ones_odd = jnp.where(band_low, 0.0, 1.0).astype(bf16)
    first_row = lax.broadcasted_iota(jnp.int32, (BF16_SUBLANES, PAIR_WIDTH), 0) == 0

    def band(var_ref, variant, r0):
        head = var_ref[variant, r0:r0 + BF16_SUBLANES, :]
        head = jnp.where(first_row, jnp.zeros_like(head), head)
        return jnp.concatenate([head, var_ref[variant, r0 + BF16_SUBLANES:r0 + 2 * WINDOW, :]], axis=0)

    def attn_scores(blk, j):
        r0 = blk * WINDOW
        q_g = jnp.concatenate(
            [q_ref[r0:r0 + WINDOW, (PAIRS_PER_KV * j + p) * PAIR_WIDTH:(PAIRS_PER_KV * j + p + 1) * PAIR_WIDTH]
             for p in range(PAIRS_PER_KV)], axis=0)
        k_g = jnp.concatenate([band(kvar_ref, 2 * j + e, r0) for e in range(2)], axis=0)
        return lax.dot_general(q_g, k_g, (((1,), (1,)), ((), ())),
                               preferred_element_type=jnp.float32)

    def attn_softmax(blk, j, s_g):
        mask = first_mask if blk == 0 else in_window
        rows = []
        for p in range(PAIRS_PER_KV):
            probs = []
            for e in range(2):
                hq = GQA_GROUP * j + 2 * p + e
                s = s_g[p * WINDOW:(p + 1) * WINDOW, e * 2 * WINDOW:(e + 1) * 2 * WINDOW]
                s = jnp.where(mask, s + bias_ref[hq], NEG_INF)
                m = jnp.max(s, axis=-1, keepdims=True)
                probs.append(jnp.exp2(s - m).astype(bf16))
            rows.append(jnp.concatenate(probs, axis=1))
        return jnp.concatenate(rows, axis=0)

    def attn_pv(blk, j, p_g):
        r0 = blk * WINDOW
        v_g = jnp.concatenate(
            [jnp.concatenate([band(vvar_ref, 2 * j + e, r0), ones], axis=1)
             for e, ones in ((0, ones_even), (1, ones_odd))], axis=0)
        r_g = _dot(p_g, v_g)
        for p in range(PAIRS_PER_KV):
            r = r_g[p * WINDOW:(p + 1) * WINDOW]
            c0 = (PAIRS_PER_KV * j + p) * PAIR_WIDTH
            o_pair = r[:, :PAIR_WIDTH] / r[:, PAIR_WIDTH:]
            yain_ref[r0:r0 + WINDOW, c0:c0 + PAIR_WIDTH] = (
                o_pair * ga_ref[r0:r0 + WINDOW, c0:c0 + PAIR_WIDTH]).astype(bf16)

    def spatial_mix():
        n_chunks = TILE // CHUNK
        for g in range(SGU_GROUPS):
            c_lo = g * SGU_GROUP_DIM
            v_g = jnp.concatenate(
                [vn_ref[c * CHUNK:(c + 1) * CHUNK, c_lo:c_lo + SGU_GROUP_DIM] for c in range(n_chunks)], axis=1)
            m_g = _dot(wtril_ref[g], v_g) + b_sp_ref[g]
            for c in range(n_chunks):
                rows = slice(c * CHUNK, (c + 1) * CHUNK)
                cols = slice(c_lo, c_lo + SGU_GROUP_DIM)
                ybin_ref[rows, cols] = (m_g[:, c * SGU_GROUP_DIM:(c + 1) * SGU_GROUP_DIM]
                                        * t_ref[rows, cols]).astype(bf16)

    def job_v():
        v_act = _gelu_exact(proj(VB_OFF, SGU_WIDTH))
        mu = jnp.mean(v_act, axis=-1, keepdims=True)
        vc = v_act - mu
        vn_ref[...] = (vc * lax.rsqrt(jnp.mean(vc * vc, axis=-1, keepdims=True) + EPS) * lng_ref[...]
                       + lnb_ref[...]).astype(bf16)

    def job_u():
        gu_ref[...] = _gelu_exact(proj(U_OFF, SGU_WIDTH)).astype(bf16)

    def job_gate_b():
        t_ref[...] = (_silu(proj(GB_OFF, SGU_WIDTH)) * gu_ref[...]).astype(bf16)

    def job_merge_b():
        sb_ref[...] = _sigmoid(proj(MB_OFF, D_MODEL)).astype(bf16)

    def job_branch_b():
        tb_ref[...] = _dot(ybin_ref[...], w_b_ref[...]) * sb_ref[...]

    def job_merge_a():
        sa_ref[...] = _sigmoid(proj(MA_OFF, D_MODEL)).astype(bf16)

    def job_gate_a():
        ga_ref[...] = _silu(proj(GA_OFF, ATTN_WIDTH)).astype(bf16)

    light_jobs = [job_gate_a, job_merge_b, job_merge_a]
    groups = [(blk, j) for blk in range(BLOCKS_PER_TILE) for j in range(N_KV_HEADS)]
    s_g = attn_scores(*groups[0])
    for gi, group in enumerate(groups):
        if light_jobs:
            light_jobs.pop(0)()
        p_g = attn_softmax(*group, s_g)
        if gi + 1 < len(groups):
            s_g = attn_scores(*groups[gi + 1])
        attn_pv(*group, p_g)
    for job in light_jobs:
        job()

    for i in range(2 * N_KV_HEADS):
        kvar_ref[i, 0:WINDOW, :] = kvar_ref[i, TILE:TILE + WINDOW, :]
        vvar_ref[i, 0:WINDOW, :] = vvar_ref[i, TILE:TILE + WINDOW, :]

    job_v()
    job_u()
    job_gate_b()
    spatial_mix()
    y_a = _dot(yain_ref[...], w_a_ref[...])
    job_branch_b()
    merged_ref[...] = (y_a * sa_ref[...] + tb_ref[...]).astype(bf16)


def _resident(shape):
    return pl.BlockSpec(shape, lambda s, *_: (0,) * len(shape), pipeline_mode=pl.Buffered(1))


def _layer_slice(layer, shape):
    return pl.BlockSpec((None,) + shape, lambda s, *_: (layer,) + (0,) * len(shape),
                        pipeline_mode=pl.Buffered(1))


def _layer(layer, x, rel_bias, sinks, bucket, pre, post, lng, lnb, w_sp, b_sp, weights, next_weights):
    B, S, D = x.shape
    w_in, w_a, w_b, w_o = weights
    assert D == D_MODEL and S % TILE == 0 and w_in.shape == (1, D_MODEL, IN_COLS)
    tiles_per_seq = S // TILE
    n_tiles = B * tiles_per_seq
    cast_blocks = D_MODEL // CAST_ROWS
    assert n_tiles >= 2 * cast_blocks or not next_weights

    def tile_index(tile):
        return (tile // tiles_per_seq, tile % tiles_per_seq, 0)

    def cast_block(s):
        return jnp.minimum(s // (n_tiles // cast_blocks), cast_blocks - 1)

    x_spec = pl.BlockSpec((1, TILE, D), lambda s, *_: tile_index(jnp.minimum(s, n_tiles - 1)))
    o_spec = pl.BlockSpec((1, TILE, D), lambda s, *_: tile_index(jnp.maximum(s - 1, 0)))
    cast_in_specs = [pl.BlockSpec((None, CAST_ROWS, w.shape[2]), lambda s, *_: (layer + 1, cast_block(s), 0))
                     for w in next_weights]
    cast_out_specs = [pl.BlockSpec((None, CAST_ROWS, w.shape[2]), lambda s, *_: (0, cast_block(s), 0))
                      for w in next_weights]
    cast_out_shapes = [jax.ShapeDtypeStruct((1,) + w.shape[1:], jnp.bfloat16) for w in next_weights]
    stage_bf16 = pltpu.VMEM((TILE, D), jnp.bfloat16)
    stage_f32 = pltpu.VMEM((TILE, D), jnp.float32)
    grid_spec = pltpu.PrefetchScalarGridSpec(
        num_scalar_prefetch=2,
        grid=(n_tiles + 1,),
        in_specs=[
            x_spec,
            o_spec,
            _resident((WINDOW, 2 * WINDOW)),
            _layer_slice(layer, (1, D)), _layer_slice(layer, (1, D)),
            _layer_slice(layer, (1, SGU_WIDTH)), _layer_slice(layer, (1, SGU_WIDTH)),
            _layer_slice(0, (D, IN_COLS)),
            _layer_slice(layer, (SGU_GROUPS, CHUNK, CHUNK)),
            _layer_slice(layer, (SGU_GROUPS, CHUNK, 1)),
            _layer_slice(0, (ATTN_WIDTH, D)), _layer_slice(0, (SGU_WIDTH, D)),
            _layer_slice(0, (D, D)),
        ] + cast_in_specs,
        out_specs=[o_spec] + cast_out_specs,
        scratch_shapes=[
            pltpu.VMEM((N_Q_HEADS, WINDOW, 2 * WINDOW), jnp.float32),
            pltpu.VMEM((SGU_GROUPS, CHUNK, CHUNK), jnp.bfloat16),
            pltpu.VMEM((2 * N_KV_HEADS, TILE + WINDOW, PAIR_WIDTH), jnp.bfloat16),
            pltpu.VMEM((2 * N_KV_HEADS, TILE + WINDOW, PAIR_WIDTH), jnp.bfloat16),
            stage_bf16,
            stage_bf16, stage_bf16,
            stage_bf16, stage_bf16,
            stage_bf16, stage_bf16, stage_bf16,
            stage_bf16, stage_bf16, stage_bf16,
            stage_f32,
        ],
    )
    outs = pl.pallas_call(
        functools.partial(_layer_kernel, tiles_per_seq=tiles_per_seq, layer=layer, n_cast=len(next_weights)),
        out_shape=[jax.ShapeDtypeStruct(x.shape, x.dtype)] + cast_out_shapes,
        grid_spec=grid_spec,
        compiler_params=pltpu.CompilerParams(
            dimension_semantics=("arbitrary",),
            vmem_limit_bytes=V7X_VMEM_LIMIT_BYTES),
        name="hybrid_layer",
    )(rel_bias, sinks, x, x, bucket, pre, post, lng, lnb, w_in, w_sp, b_sp, w_a, w_b, w_o, *next_weights)
    return outs[0], tuple(outs[1:])


def kernel(x, rel_bias, pre_norm, post_norm, w_in, sinks, sgu_ln_g, sgu_ln_b, w_spatial, b_spatial,
           w_branch_a, w_branch_b, w_out):
    depth = w_in.shape[0]
    bucket = jnp.asarray(_t5_bucket_table())
    small = (rel_bias, sinks, bucket,
             pre_norm[:, None], post_norm[:, None], sgu_ln_g[:, None], sgu_ln_b[:, None],
             w_spatial, b_spatial[..., None])
    stacked = (w_in, w_branch_a, w_branch_b, w_out)
    weights = tuple(w[:1].astype(jnp.bfloat16) for w in stacked)
    for l in range(depth):
        x, weights = _layer(l, x, *small, weights, stacked if l + 1 < depth else ())
    return x
```

```python
import functools

import jax
import jax.numpy as jnp
import numpy as np
from jax import lax
from jax.experimental import pallas as pl
from jax.experimental.pallas import tpu as pltpu

D_MODEL = 1024
HEAD_DIM = 64
N_Q_HEADS = 16
N_KV_HEADS = 2
GQA_GROUP = N_Q_HEADS // N_KV_HEADS
WINDOW = 128
ATTN_WIDTH = N_Q_HEADS * HEAD_DIM
KV_WIDTH = N_KV_HEADS * HEAD_DIM
PAIR_WIDTH = 2 * HEAD_DIM
PAIRS_PER_KV = GQA_GROUP // 2
N_BUCKETS = 32
MAX_DISTANCE = 128
SGU_WIDTH = D_MODEL
SGU_GROUPS = 8
SGU_GROUP_DIM = SGU_WIDTH // SGU_GROUPS
CHUNK = 128
EPS = 1e-6
NEG_INF = -1e30
LOG2_E = float(np.log2(np.e))
BF16_SUBLANES = 16
CAST_ROWS = BF16_SUBLANES

_SPLITS = (ATTN_WIDTH, KV_WIDTH, KV_WIDTH, ATTN_WIDTH, SGU_WIDTH, SGU_WIDTH, SGU_WIDTH, D_MODEL, D_MODEL)
_OFFS = tuple(int(o) for o in np.cumsum((0,) + _SPLITS))
IN_COLS = _OFFS[-1]
(Q_OFF, K_OFF, V_OFF, GA_OFF, U_OFF, VB_OFF, GB_OFF, MA_OFF, MB_OFF) = _OFFS[:-1]

TILE = 256
SUBTILES = 2
STEP_ROWS = SUBTILES * TILE
BLOCKS_PER_TILE = TILE // WINDOW
V7X_VMEM_LIMIT_BYTES = 56 * 1024 * 1024


def _t5_bucket_table():
    qi = np.arange(WINDOW)[:, None]
    kj = np.arange(2 * WINDOW)[None, :]
    dist = np.clip(qi + WINDOW - kj, 0, None)
    max_exact = N_BUCKETS // 2
    large = max_exact + (np.log(np.maximum(dist, max_exact) / max_exact)
                         / np.log(MAX_DISTANCE / max_exact)
                         * (N_BUCKETS - max_exact)).astype(np.int32)
    large = np.minimum(large, N_BUCKETS - 1)
    return np.where(dist < max_exact, dist, large).astype(np.int32)


def _dot(a, b):
    return jnp.dot(a, b, preferred_element_type=jnp.float32)


def _gelu_exact(x):
    return 0.5 * x * (1.0 + lax.erf(x * np.float32(np.sqrt(0.5))))


def _sigmoid(x):
    return 0.5 * (1.0 + jnp.tanh(0.5 * x))


def _silu(x):
    half = 0.5 * x
    return half * (1.0 + jnp.tanh(half))


def _layer_kernel(rel_bias_ref, sinks_ref, *refs, blocks_per_seq, layer, n_cast):
    (x_ref, xprev_ref, bucket_ref, pre_ref, post_ref, lng_ref, lnb_ref,
     w_in_ref, w_sp_ref, b_sp_ref, w_a_ref, w_b_ref, w_o_ref) = refs[:13]
    cast_src = refs[13:13 + n_cast]
    o_ref = refs[13 + n_cast]
    cast_dst = refs[14 + n_cast:14 + 2 * n_cast]
    (bias_ref, wtril_ref, kvar_ref, vvar_ref, merged_ref,
     h_ref, q_ref, ga_ref, yain_ref, vn_ref, gu_ref, t_ref, ybin_ref, sb_ref, sa_ref, tb_ref) = refs[14 + 2 * n_cast:]
    step = pl.program_id(0)
    seq_start = jnp.minimum(step, pl.num_programs(0) - 2) % blocks_per_seq == 0
    bf16 = jnp.bfloat16

    for src, dst in zip(cast_src, cast_dst):
        dst[...] = src[...].astype(bf16)

    @pl.when(step == 0)
    def _():
        merged_ref[...] = jnp.zeros_like(merged_ref)
        bucket = bucket_ref[...]
        sink_col = lax.broadcasted_iota(jnp.int32, (WINDOW, 2 * WINDOW), 1) == 0
        for h in range(N_Q_HEADS):
            acc = jnp.zeros((WINDOW, 2 * WINDOW), jnp.float32)
            for bk in range(N_BUCKETS):
                acc = jnp.where(bucket == bk, rel_bias_ref[bk, h], acc)
            bias_ref[h] = jnp.where(sink_col, sinks_ref[layer, h], acc) * LOG2_E
        row = lax.broadcasted_iota(jnp.int32, (CHUNK, CHUNK), 0)
        col = lax.broadcasted_iota(jnp.int32, (CHUNK, CHUNK), 1)
        for g in range(SGU_GROUPS):
            wtril_ref[g] = jnp.where(col <= row, w_sp_ref[g], 0.0).astype(bf16)

    @pl.when(seq_start)
    def _():
        for i in range(2 * N_KV_HEADS):
            kvar_ref[i, 0:WINDOW, :] = jnp.zeros((WINDOW, PAIR_WIDTH), bf16)
            vvar_ref[i, 0:WINDOW, :] = jnp.zeros((WINDOW, PAIR_WIDTH), bf16)

    qi = lax.broadcasted_iota(jnp.int32, (WINDOW, 2 * WINDOW), 0)
    kj = lax.broadcasted_iota(jnp.int32, (WINDOW, 2 * WINDOW), 1)
    dist = qi + WINDOW - kj
    in_window = jnp.logical_or(jnp.logical_and(dist >= 0, dist < WINDOW), kj == 0)
    seq_start_mask = jnp.logical_and(
        in_window, jnp.logical_or(jnp.logical_or(kj >= WINDOW, kj == 0), jnp.logical_not(seq_start)))
    band_low = lax.broadcasted_iota(jnp.int32, (2 * WINDOW, PAIR_WIDTH), 1) < HEAD_DIM
    ones_even = jnp.where(band_low, 1.0, 0.0).astype(bf16)
    ones_odd = jnp.where(band_low, 0.0, 1.0).astype(bf16)
    first_row = lax.broadcasted_iota(jnp.int32, (BF16_SUBLANES, PAIR_WIDTH), 0) == 0
    low = lax.broadcasted_iota(jnp.int32, (TILE, PAIR_WIDTH), 1) < HEAD_DIM

    def back(sub):
        rows = slice(sub * TILE, (sub + 1) * TILE)
        out = _dot(merged_ref[sub], w_o_ref[...])
        o_ref[0, rows, :] = (xprev_ref[0, rows, :]
                             + out * lax.rsqrt(jnp.mean(out * out, axis=-1, keepdims=True) + EPS) * post_ref[...])

    def band(var_ref, variant, r0):
        head = var_ref[variant, r0:r0 + BF16_SUBLANES, :]
        head = jnp.where(first_row, jnp.zeros_like(head), head)
        return jnp.concatenate([head, var_ref[variant, r0 + BF16_SUBLANES:r0 + 2 * WINDOW, :]], axis=0)

    def place(var_ref, t):
        zero = jnp.zeros_like(t)
        swapped = pltpu.roll(t, HEAD_DIM, 1)
        var_ref[0, WINDOW:, :] = jnp.where(low, t, zero)
        var_ref[1, WINDOW:, :] = jnp.where(low, zero, swapped)
        var_ref[2, WINDOW:, :] = jnp.where(low, swapped, zero)
        var_ref[3, WINDOW:, :] = jnp.where(low, zero, t)

    def front(sub):
        x = x_ref[0, sub * TILE:(sub + 1) * TILE, :]
        h_ref[...] = (x * lax.rsqrt(jnp.mean(x * x, axis=-1, keepdims=True) + EPS) * pre_ref[...]).astype(bf16)

        def proj(off, width):
            return _dot(h_ref[...], w_in_ref[:, off:off + width])

        q_ref[...] = (proj(Q_OFF, ATTN_WIDTH) * np.float32(HEAD_DIM ** -0.5 * LOG2_E)).astype(bf16)
        w_kv = w_in_ref[:, K_OFF:K_OFF + 2 * KV_WIDTH]
        kv = jnp.concatenate([_dot(h_ref[:TILE // 2, :], w_kv), _dot(h_ref[TILE // 2:, :], w_kv)], axis=0)
        place(kvar_ref, kv[:, :KV_WIDTH].astype(bf16))
        place(vvar_ref, kv[:, KV_WIDTH:].astype(bf16))

        def attn_scores(blk, j):
            r0 = blk * WINDOW
            q_g = jnp.concatenate(
                [q_ref[r0:r0 + WINDOW, (PAIRS_PER_KV * j + p) * PAIR_WIDTH:(PAIRS_PER_KV * j + p + 1) * PAIR_WIDTH]
                 for p in range(PAIRS_PER_KV)], axis=0)
            k_g = jnp.concatenate([band(kvar_ref, 2 * j + e, r0) for e in range(2)], axis=0)
            return lax.dot_general(q_g, k_g, (((1,), (1,)), ((), ())),
                                   preferred_element_type=jnp.float32)

        def attn_softmax(blk, j, s_g):
            mask = seq_start_mask if (sub == 0 and blk == 0) else in_window
            rows = []
            for p in range(PAIRS_PER_KV):
                probs = []
                for e in range(2):
                    hq = GQA_GROUP * j + 2 * p + e
                    s = s_g[p * WINDOW:(p + 1) * WINDOW, e * 2 * WINDOW:(e + 1) * 2 * WINDOW]
                    s = jnp.where(mask, s + bias_ref[hq], NEG_INF)
                    m = jnp.max(s, axis=-1, keepdims=True)
                    probs.append(jnp.exp2(s - m).astype(bf16))
                rows.append(jnp.concatenate(probs, axis=1))
            return jnp.concatenate(rows, axis=0)

        def attn_pv(blk, j, p_g):
            r0 = blk * WINDOW
            v_g = jnp.concatenate(
                [jnp.concatenate([band(vvar_ref, 2 * j + e, r0), ones], axis=1)
                 for e, ones in ((0, ones_even), (1, ones_odd))], axis=0)
            r_g = _dot(p_g, v_g)
            for p in range(PAIRS_PER_KV):
                r = r_g[p * WINDOW:(p + 1) * WINDOW]
                c0 = (PAIRS_PER_KV * j + p) * PAIR_WIDTH
                o_pair = r[:, :PAIR_WIDTH] / r[:, PAIR_WIDTH:]
                yain_ref[r0:r0 + WINDOW, c0:c0 + PAIR_WIDTH] = (
                    o_pair * ga_ref[r0:r0 + WINDOW, c0:c0 + PAIR_WIDTH]).astype(bf16)

        def spatial_mix():
            n_chunks = TILE // CHUNK
            for g in range(SGU_GROUPS):
                c_lo = g * SGU_GROUP_DIM
                v_g = jnp.concatenate(
                    [vn_ref[c * CHUNK:(c + 1) * CHUNK, c_lo:c_lo + SGU_GROUP_DIM] for c in range(n_chunks)],
                    axis=1)
                m_g = _dot(wtril_ref[g], v_g) + b_sp_ref[g]
                for c in range(n_chunks):
                    rows = slice(c * CHUNK, (c + 1) * CHUNK)
                    cols = slice(c_lo, c_lo + SGU_GROUP_DIM)
                    ybin_ref[rows, cols] = (m_g[:, c * SGU_GROUP_DIM:(c + 1) * SGU_GROUP_DIM]
                                            * t_ref[rows, cols]).astype(bf16)

        def job_v():
            v_act = _gelu_exact(proj(VB_OFF, SGU_WIDTH))
            mu = jnp.mean(v_act, axis=-1, keepdims=True)
            vc = v_act - mu
            vn_ref[...] = (vc * lax.rsqrt(jnp.mean(vc * vc, axis=-1, keepdims=True) + EPS) * lng_ref[...]
                           + lnb_ref[...]).astype(bf16)

        def job_u():
            gu_ref[...] = _gelu_exact(proj(U_OFF, SGU_WIDTH)).astype(bf16)

        def job_gate_b():
            t_ref[...] = (_silu(proj(GB_OFF, SGU_WIDTH)) * gu_ref[...]).astype(bf16)

        def job_merge_b():
            sb_ref[...] = _sigmoid(proj(MB_OFF, D_MODEL)).astype(bf16)

        def job_branch_b():
            tb_ref[...] = _dot(ybin_ref[...], w_b_ref[...]) * sb_ref[...]

        def job_merge_a():
            sa_ref[...] = _sigmoid(proj(MA_OFF, D_MODEL)).astype(bf16)

        dense_jobs = [job_v, job_u, job_gate_b, spatial_mix, job_merge_b, job_branch_b, job_merge_a]
        groups = [(blk, j) for blk in range(BLOCKS_PER_TILE) for j in range(N_KV_HEADS)]
        ga_ref[...] = _silu(proj(GA_OFF, ATTN_WIDTH)).astype(bf16)
        s_g = attn_scores(*groups[0])
        for gi, group in enumerate(groups):
            if dense_jobs:
                dense_jobs.pop(0)()
            p_g = attn_softmax(*group, s_g)
            if gi + 1 < len(groups):
                s_g = attn_scores(*groups[gi + 1])
            attn_pv(*group, p_g)

        for i in range(2 * N_KV_HEADS):
            kvar_ref[i, 0:WINDOW, :] = kvar_ref[i, TILE:TILE + WINDOW, :]
            vvar_ref[i, 0:WINDOW, :] = vvar_ref[i, TILE:TILE + WINDOW, :]

        for job in dense_jobs:
            job()
        merged_ref[sub] = (_dot(yain_ref[...], w_a_ref[...]) * sa_ref[...] + tb_ref[...]).astype(bf16)

    for sub in range(SUBTILES):
        back(sub)
        front(sub)


def _resident(shape):
    return pl.BlockSpec(shape, lambda s, *_: (0,) * len(shape), pipeline_mode=pl.Buffered(1))


def _layer_slice(layer, shape):
    return pl.BlockSpec((None,) + shape, lambda s, *_: (layer,) + (0,) * len(shape),
                        pipeline_mode=pl.Buffered(1))


def _layer(layer, x, rel_bias, sinks, bucket, pre, post, lng, lnb, w_sp, b_sp, weights, next_weights):
    B, S, D = x.shape
    w_in, w_a, w_b, w_o = weights
    assert D == D_MODEL and S % STEP_ROWS == 0 and w_in.shape == (1, D_MODEL, IN_COLS)
    blocks_per_seq = S // STEP_ROWS
    n_blocks = B * blocks_per_seq
    cast_blocks = D_MODEL // CAST_ROWS
    assert n_blocks >= cast_blocks or not next_weights

    def block_index(block):
        return (block // blocks_per_seq, block % blocks_per_seq, 0)

    def cast_block(s):
        return jnp.minimum(s, cast_blocks - 1)

    x_spec = pl.BlockSpec((1, STEP_ROWS, D), lambda s, *_: block_index(jnp.minimum(s, n_blocks - 1)))
    o_spec = pl.BlockSpec((1, STEP_ROWS, D), lambda s, *_: block_index(jnp.maximum(s - 1, 0)))
    cast_in_specs = [pl.BlockSpec((None, CAST_ROWS, w.shape[2]), lambda s, *_: (layer + 1, cast_block(s), 0))
                     for w in next_weights]
    cast_out_specs = [pl.BlockSpec((None, CAST_ROWS, w.shape[2]), lambda s, *_: (0, cast_block(s), 0))
                      for w in next_weights]
    cast_out_shapes = [jax.ShapeDtypeStruct((1,) + w.shape[1:], jnp.bfloat16) for w in next_weights]
    stage_bf16 = pltpu.VMEM((TILE, D), jnp.bfloat16)
    stage_f32 = pltpu.VMEM((TILE, D), jnp.float32)
    grid_spec = pltpu.PrefetchScalarGridSpec(
        num_scalar_prefetch=2,
        grid=(n_blocks + 1,),
        in_specs=[
            x_spec,
            o_spec,
            _resident((WINDOW, 2 * WINDOW)),
            _layer_slice(layer, (1, D)), _layer_slice(layer, (1, D)),
            _layer_slice(layer, (1, SGU_WIDTH)), _layer_slice(layer, (1, SGU_WIDTH)),
            _layer_slice(0, (D, IN_COLS)),
            _layer_slice(layer, (SGU_GROUPS, CHUNK, CHUNK)),
            _layer_slice(layer, (SGU_GROUPS, CHUNK, 1)),
            _layer_slice(0, (ATTN_WIDTH, D)), _layer_slice(0, (SGU_WIDTH, D)),
            _layer_slice(0, (D, D)),
        ] + cast_in_specs,
        out_specs=[o_spec] + cast_out_specs,
        scratch_shapes=[
            pltpu.VMEM((N_Q_HEADS, WINDOW, 2 * WINDOW), jnp.float32),
            pltpu.VMEM((SGU_GROUPS, CHUNK, CHUNK), jnp.bfloat16),
            pltpu.VMEM((2 * N_KV_HEADS, TILE + WINDOW, PAIR_WIDTH), jnp.bfloat16),
            pltpu.VMEM((2 * N_KV_HEADS, TILE + WINDOW, PAIR_WIDTH), jnp.bfloat16),
            pltpu.VMEM((SUBTILES, TILE, D), jnp.bfloat16),
            stage_bf16, stage_bf16,
            stage_bf16, stage_bf16,
            stage_bf16, stage_bf16, stage_bf16,
            stage_bf16, stage_bf16, stage_bf16,
            stage_f32,
        ],
    )
    outs = pl.pallas_call(
        functools.partial(_layer_kernel, blocks_per_seq=blocks_per_seq, layer=layer, n_cast=len(next_weights)),
        out_shape=[jax.ShapeDtypeStruct(x.shape, x.dtype)] + cast_out_shapes,
        grid_spec=grid_spec,
        compiler_params=pltpu.CompilerParams(
            dimension_semantics=("arbitrary",),
            vmem_limit_bytes=V7X_VMEM_LIMIT_BYTES),
        name="hybrid_layer",
    )(rel_bias, sinks, x, x, bucket, pre, post, lng, lnb, w_in, w_sp, b_sp, w_a, w_b, w_o, *next_weights)
    return outs[0], tuple(outs[1:])


def kernel(x, rel_bias, pre_norm, post_norm, w_in, sinks, sgu_ln_g, sgu_ln_b, w_spatial, b_spatial,
           w_branch_a, w_branch_b, w_out):
    depth = w_in.shape[0]
    bucket = jnp.asarray(_t5_bucket_table())
    small = (rel_bias, sinks, bucket,
             pre_norm[:, None], post_norm[:, None], sgu_ln_g[:, None], sgu_ln_b[:, None],
             w_spatial, b_spatial[..., None])
    stacked = (w_in, w_branch_a, w_branch_b, w_out)
    weights = tuple(w[:1].astype(jnp.bfloat16) for w in stacked)
    for l in range(depth):
        x, weights = _layer(l, x, *small, weights, stacked if l + 1 < depth else ())
    return x
```

```python
import functools

import jax
import jax.numpy as jnp
import numpy as np
from jax import lax
from jax.experimental import pallas as pl
from jax.experimental.pallas import tpu as pltpu

D_MODEL = 1024
HEAD_DIM = 64
N_Q_HEADS = 16
N_KV_HEADS = 2
GQA_GROUP = N_Q_HEADS // N_KV_HEADS
WINDOW = 128
ATTN_WIDTH = N_Q_HEADS * HEAD_DIM
KV_WIDTH = N_KV_HEADS * HEAD_DIM
PAIR_WIDTH = 2 * HEAD_DIM
PAIRS_PER_KV = GQA_GROUP // 2
N_BUCKETS = 32
MAX_DISTANCE = 128
SGU_WIDTH = D_MODEL
SGU_GROUPS = 8
SGU_GROUP_DIM = SGU_WIDTH // SGU_GROUPS
CHUNK = 128
EPS = 1e-6
NEG_INF = -1e30
LOG2_E = float(np.log2(np.e))
BF16_SUBLANES = 16
CAST_ROWS = BF16_SUBLANES

_SPLITS = (ATTN_WIDTH, KV_WIDTH, KV_WIDTH, ATTN_WIDTH, SGU_WIDTH, SGU_WIDTH, SGU_WIDTH, D_MODEL, D_MODEL)
_OFFS = tuple(int(o) for o in np.cumsum((0,) + _SPLITS))
IN_COLS = _OFFS[-1]
(Q_OFF, K_OFF, V_OFF, GA_OFF, U_OFF, VB_OFF, GB_OFF, MA_OFF, MB_OFF) = _OFFS[:-1]

TILE = 256
BLOCKS_PER_TILE = TILE // WINDOW
V7X_VMEM_LIMIT_BYTES = 56 * 1024 * 1024


def _t5_bucket_table():
    qi = np.arange(WINDOW)[:, None]
    kj = np.arange(2 * WINDOW)[None, :]
    dist = np.clip(qi + WINDOW - kj, 0, None)
    max_exact = N_BUCKETS // 2
    large = max_exact + (np.log(np.maximum(dist, max_exact) / max_exact)
                         / np.log(MAX_DISTANCE / max_exact)
                         * (N_BUCKETS - max_exact)).astype(np.int32)
    large = np.minimum(large, N_BUCKETS - 1)
    return np.where(dist < max_exact, dist, large).astype(np.int32)


def _dot(a, b):
    return jnp.dot(a, b, preferred_element_type=jnp.float32)


def _gelu_exact(x):
    return 0.5 * x * (1.0 + lax.erf(x * np.float32(np.sqrt(0.5))))


def _sigmoid(x):
    return 0.5 * (1.0 + jnp.tanh(0.5 * x))


def _silu(x):
    half = 0.5 * x
    return half * (1.0 + jnp.tanh(half))


def _layer_kernel(rel_bias_ref, sinks_ref, *refs, tiles_per_seq, layer, n_cast):
    (x_ref, xprev_ref, bucket_ref, pre_ref, post_ref, lng_ref, lnb_ref,
     w_in_ref, w_sp_ref, b_sp_ref, w_a_ref, w_b_ref, w_o_ref) = refs[:13]
    cast_src = refs[13:13 + n_cast]
    o_ref = refs[13 + n_cast]
    cast_dst = refs[14 + n_cast:14 + 2 * n_cast]
    (bias_ref, wtril_ref, kvar_ref, vvar_ref, merged_ref,
     h_ref, q_ref, ga_ref, yain_ref, vn_ref, gu_ref, t_ref, ybin_ref, sb_ref, sa_ref, tb_ref) = refs[14 + 2 * n_cast:]
    step = pl.program_id(0)
    t_idx = jnp.minimum(step, pl.num_programs(0) - 2) % tiles_per_seq
    bf16 = jnp.bfloat16

    for src, dst in zip(cast_src, cast_dst):
        dst[...] = src[...].astype(bf16)

    @pl.when(step == 0)
    def _():
        merged_ref[...] = jnp.zeros_like(merged_ref)
        bucket = bucket_ref[...]
        sink_col = lax.broadcasted_iota(jnp.int32, (WINDOW, 2 * WINDOW), 1) == 0
        for h in range(N_Q_HEADS):
            acc = jnp.zeros((WINDOW, 2 * WINDOW), jnp.float32)
            for bk in range(N_BUCKETS):
                acc = jnp.where(bucket == bk, rel_bias_ref[bk, h], acc)
            bias_ref[h] = jnp.where(sink_col, sinks_ref[layer, h], acc) * LOG2_E
        row = lax.broadcasted_iota(jnp.int32, (CHUNK, CHUNK), 0)
        col = lax.broadcasted_iota(jnp.int32, (CHUNK, CHUNK), 1)
        for g in range(SGU_GROUPS):
            wtril_ref[g] = jnp.where(col <= row, w_sp_ref[g], 0.0).astype(bf16)

    @pl.when(t_idx == 0)
    def _():
        for i in range(2 * N_KV_HEADS):
            kvar_ref[i, 0:WINDOW, :] = jnp.zeros((WINDOW, PAIR_WIDTH), bf16)
            vvar_ref[i, 0:WINDOW, :] = jnp.zeros((WINDOW, PAIR_WIDTH), bf16)

    out = _dot(merged_ref[...], w_o_ref[...])
    o_ref[0] = (xprev_ref[0]
                + out * lax.rsqrt(jnp.mean(out * out, axis=-1, keepdims=True) + EPS) * post_ref[...])

    x = x_ref[0]
    h_ref[...] = (x * lax.rsqrt(jnp.mean(x * x, axis=-1, keepdims=True) + EPS) * pre_ref[...]).astype(bf16)

    def proj(off, width):
        return _dot(h_ref[...], w_in_ref[:, off:off + width])

    q_ref[...] = (proj(Q_OFF, ATTN_WIDTH) * np.float32(HEAD_DIM ** -0.5 * LOG2_E)).astype(bf16)
    kv = proj(K_OFF, 2 * KV_WIDTH)
    low = lax.broadcasted_iota(jnp.int32, (TILE, PAIR_WIDTH), 1) < HEAD_DIM

    def place(var_ref, t):
        zero = jnp.zeros_like(t)
        swapped = pltpu.roll(t, HEAD_DIM, 1)
        var_ref[0, WINDOW:, :] = jnp.where(low, t, zero)
        var_ref[1, WINDOW:, :] = jnp.where(low, zero, swapped)
        var_ref[2, WINDOW:, :] = jnp.where(low, swapped, zero)
        var_ref[3, WINDOW:, :] = jnp.where(low, zero, t)

    place(kvar_ref, kv[:, :KV_WIDTH].astype(bf16))
    place(vvar_ref, kv[:, KV_WIDTH:].astype(bf16))

    qi = lax.broadcasted_iota(jnp.int32, (WINDOW, 2 * WINDOW), 0)
    kj = lax.broadcasted_iota(jnp.int32, (WINDOW, 2 * WINDOW), 1)
    dist = qi + WINDOW - kj
    in_window = jnp.logical_or(jnp.logical_and(dist >= 0, dist < WINDOW), kj == 0)
    first_mask = jnp.logical_and(in_window, jnp.logical_or(jnp.logical_or(kj >= WINDOW, kj == 0), t_idx > 0))
    band_low = lax.broadcasted_iota(jnp.int32, (2 * WINDOW, PAIR_WIDTH), 1) < HEAD_DIM
    ones_even = jnp.where(band_low, 1.0, 0.0).astype(bf16)
    ones_odd = jnp.where(band_low, 0.0, 1.0).astype(bf16)
    first_row = lax.broadcasted_iota(jnp.int32, (BF16_SUBLANES, PAIR_WIDTH), 0) == 0

    def band(var_ref, variant, r0):
        head = var_ref[variant, r0:r0 + BF16_SUBLANES, :]
        head = jnp.where(first_row, jnp.zeros_like(head), head)
        return jnp.concatenate([head, var_ref[variant, r0 + BF16_SUBLANES:r0 + 2 * WINDOW, :]], axis=0)

    def attn_scores(blk, j):
        r0 = blk * WINDOW
        q_g = jnp.concatenate(
            [q_ref[r0:r0 + WINDOW, (PAIRS_PER_KV * j + p) * PAIR_WIDTH:(PAIRS_PER_KV * j + p + 1) * PAIR_WIDTH]
             for p in range(PAIRS_PER_KV)], axis=0)
        k_g = jnp.concatenate([band(kvar_ref, 2 * j + e, r0) for e in range(2)], axis=0)
        return lax.dot_general(q_g, k_g, (((1,), (1,)), ((), ())),
                               preferred_element_type=jnp.float32)

    def attn_softmax(blk, j, s_g):
        mask = first_mask if blk == 0 else in_window
        rows = []
        for p in range(PAIRS_PER_KV):
            probs = []
            for e in range(2):
                hq = GQA_GROUP * j + 2 * p + e
                s = s_g[p * WINDOW:(p + 1) * WINDOW, e * 2 * WINDOW:(e + 1) * 2 * WINDOW]
                s = jnp.where(mask, s + bias_ref[hq], NEG_INF)
                m = jnp.max(s, axis=-1, keepdims=True)
                probs.append(jnp.exp2(s - m).astype(bf16))
            rows.append(jnp.concatenate(probs, axis=1))
        return jnp.concatenate(rows, axis=0)

    def attn_pv(blk, j, p_g):
        r0 = blk * WINDOW
        v_g = jnp.concatenate(
            [jnp.concatenate([band(vvar_ref, 2 * j + e, r0), ones], axis=1)
             for e, ones in ((0, ones_even), (1, ones_odd))], axis=0)
        r_g = _dot(p_g, v_g)
        for p in range(PAIRS_PER_KV):
            r = r_g[p * WINDOW:(p + 1) * WINDOW]
            c0 = (PAIRS_PER_KV * j + p) * PAIR_WIDTH
            o_pair = r[:, :PAIR_WIDTH] / r[:, PAIR_WIDTH:]
            yain_ref[r0:r0 + WINDOW, c0:c0 + PAIR_WIDTH] = (
                o_pair * ga_ref[r0:r0 + WINDOW, c0:c0 + PAIR_WIDTH]).astype(bf16)

    def spatial_mix():
        n_chunks = TILE // CHUNK
        for g in range(SGU_GROUPS):
            c_lo = g * SGU_GROUP_DIM
            v_g = jnp.concatenate(
                [vn_ref[c * CHUNK:(c + 1) * CHUNK, c_lo:c_lo + SGU_GROUP_DIM] for c in range(n_chunks)], axis=1)
            m_g = _dot(wtril_ref[g], v_g) + b_sp_ref[g]
            for c in range(n_chunks):
                rows = slice(c * CHUNK, (c + 1) * CHUNK)
                cols = slice(c_lo, c_lo + SGU_GROUP_DIM)
                ybin_ref[rows, cols] = (m_g[:, c * SGU_GROUP_DIM:(c + 1) * SGU_GROUP_DIM]
                                        * t_ref[rows, cols]).astype(bf16)

    def job_v():
        v_act = _gelu_exact(proj(VB_OFF, SGU_WIDTH))
        mu = jnp.mean(v_act, axis=-1, keepdims=True)
        vc = v_act - mu
        vn_ref[...] = (vc * lax.rsqrt(jnp.mean(vc * vc, axis=-1, keepdims=True) + EPS) * lng_ref[...]
                       + lnb_ref[...]).astype(bf16)

    def job_u():
        gu_ref[...] = _gelu_exact(proj(U_OFF, SGU_WIDTH)).astype(bf16)

    def job_gate_b():
        t_ref[...] = (_silu(proj(GB_OFF, SGU_WIDTH)) * gu_ref[...]).astype(bf16)

    def job_merge_b():
        sb_ref[...] = _sigmoid(proj(MB_OFF, D_MODEL)).astype(bf16)

    def job_branch_b():
        tb_ref[...] = _dot(ybin_ref[...], w_b_ref[...]) * sb_ref[...]

    def job_merge_a():
        sa_ref[...] = _sigmoid(proj(MA_OFF, D_MODEL)).astype(bf16)

    dense_jobs = [job_v, job_u, job_gate_b, spatial_mix, job_merge_b, job_branch_b, job_merge_a]
    groups = [(blk, j) for blk in range(BLOCKS_PER_TILE) for j in range(N_KV_HEADS)]
    ga_ref[...] = _silu(proj(GA_OFF, ATTN_WIDTH)).astype(bf16)
    s_g = attn_scores(*groups[0])
    for gi, group in enumerate(groups):
        if dense_jobs:
            dense_jobs.pop(0)()
        p_g = attn_softmax(*group, s_g)
        if gi + 1 < len(groups):
            s_g = attn_scores(*groups[gi + 1])
        attn_pv(*group, p_g)

    for i in range(2 * N_KV_HEADS):
        kvar_ref[i, 0:WINDOW, :] = kvar_ref[i, TILE:TILE + WINDOW, :]
        vvar_ref[i, 0:WINDOW, :] = vvar_ref[i, TILE:TILE + WINDOW, :]

    for job in dense_jobs:
        job()
    merged_ref[...] = (_dot(yain_ref[...], w_a_ref[...]) * sa_ref[...] + tb_ref[...]).astype(bf16)


def _resident(shape):
    return pl.BlockSpec(shape, lambda s, *_: (0,) * len(shape), pipeline_mode=pl.Buffered(1))


def _layer_slice(layer, shape):
    return pl.BlockSpec((None,) + shape, lambda s, *_: (layer,) + (0,) * len(shape),
                        pipeline_mode=pl.Buffered(1))


def _layer(layer, x, rel_bias, sinks, bucket, pre, post, lng, lnb, w_sp, b_sp, weights, next_weights):
    B, S, D = x.shape
    w_in, w_a, w_b, w_o = weights
    assert D == D_MODEL and S % TILE == 0 and w_in.shape == (1, D_MODEL, IN_COLS)
    tiles_per_seq = S // TILE
    n_tiles = B * tiles_per_seq
    cast_blocks = D_MODEL // CAST_ROWS
    assert n_tiles >= 2 * cast_blocks or not next_weights

    def tile_index(tile):
        return (tile // tiles_per_seq, tile % tiles_per_seq, 0)

    def cast_block(s):
        return jnp.minimum(s // (n_tiles // cast_blocks), cast_blocks - 1)

    x_spec = pl.BlockSpec((1, TILE, D), lambda s, *_: tile_index(jnp.minimum(s, n_tiles - 1)))
    o_spec = pl.BlockSpec((1, TILE, D), lambda s, *_: tile_index(jnp.maximum(s - 1, 0)))
    cast_in_specs = [pl.BlockSpec((None, CAST_ROWS, w.shape[2]), lambda s, *_: (layer + 1, cast_block(s), 0))
                     for w in next_weights]
    cast_out_specs = [pl.BlockSpec((None, CAST_ROWS, w.shape[2]), lambda s, *_: (0, cast_block(s), 0))
                      for w in next_weights]
    cast_out_shapes = [jax.ShapeDtypeStruct((1,) + w.shape[1:], jnp.bfloat16) for w in next_weights]
    stage_bf16 = pltpu.VMEM((TILE, D), jnp.bfloat16)
    stage_f32 = pltpu.VMEM((TILE, D), jnp.float32)
    grid_spec = pltpu.PrefetchScalarGridSpec(
        num_scalar_prefetch=2,
        grid=(n_tiles + 1,),
        in_specs=[
            x_spec,
            o_spec,
            _resident((WINDOW, 2 * WINDOW)),
            _layer_slice(layer, (1, D)), _layer_slice(layer, (1, D)),
            _layer_slice(layer, (1, SGU_WIDTH)), _layer_slice(layer, (1, SGU_WIDTH)),
            _layer_slice(0, (D, IN_COLS)),
            _layer_slice(layer, (SGU_GROUPS, CHUNK, CHUNK)),
            _layer_slice(layer, (SGU_GROUPS, CHUNK, 1)),
            _layer_slice(0, (ATTN_WIDTH, D)), _layer_slice(0, (SGU_WIDTH, D)),
            _layer_slice(0, (D, D)),
        ] + cast_in_specs,
        out_specs=[o_spec] + cast_out_specs,
        scratch_shapes=[
            pltpu.VMEM((N_Q_HEADS, WINDOW, 2 * WINDOW), jnp.float32),
            pltpu.VMEM((SGU_GROUPS, CHUNK, CHUNK), jnp.bfloat16),
            pltpu.VMEM((2 * N_KV_HEADS, TILE + WINDOW, PAIR_WIDTH), jnp.bfloat16),
            pltpu.VMEM((2 * N_KV_HEADS, TILE + WINDOW, PAIR_WIDTH), jnp.bfloat16),
            stage_bf16,
            stage_bf16, stage_bf16,
            stage_bf16, stage_bf16,
            stage_bf16, stage_bf16, stage_bf16,
            stage_bf16, stage_bf16, stage_bf16,
            stage_f32,
        ],
    )
    outs = pl.pallas_call(
        functools.partial(_layer_kernel, tiles_per_seq=tiles_per_seq, layer=layer, n_cast=len(next_weights)),
        out_shape=[jax.ShapeDtypeStruct(x.shape, x.dtype)] + cast_out_shapes,
        grid_spec=grid_spec,
        compiler_params=pltpu.CompilerParams(
            dimension_semantics=("arbitrary",),
            vmem_limit_bytes=V7X_VMEM_LIMIT_BYTES),
        name="hybrid_layer",
    )(rel_bias, sinks, x, x, bucket, pre, post, lng, lnb, w_in, w_sp, b_sp, w_a, w_b, w_o, *next_weights)
    return outs[0], tuple(outs[1:])


def kernel(x, rel_bias, pre_norm, post_norm, w_in, sinks, sgu_ln_g, sgu_ln_b, w_spatial, b_spatial,
           w_branch_a, w_branch_b, w_out):
    depth = w_in.shape[0]
    bucket = jnp.asarray(_t5_bucket_table())
    small = (rel_bias, sinks, bucket,
             pre_norm[:, None], post_norm[:, None], sgu_ln_g[:, None], sgu_ln_b[:, None],
             w_spatial, b_spatial[..., None])
    stacked = (w_in, w_branch_a, w_branch_b, w_out)
    weights = tuple(w[:1].astype(jnp.bfloat16) for w in stacked)
    for l in range(depth):
        x, weights = _layer(l, x, *small, weights, stacked if l + 1 < depth else ())
    return x
```

```python
import functools

import jax
import jax.numpy as jnp
import numpy as np
from jax import lax
from jax.experimental import pallas as pl
from jax.experimental.pallas import tpu as pltpu

D_MODEL = 1024
HEAD_DIM = 64
N_Q_HEADS = 16
N_KV_HEADS = 2
GQA_GROUP = N_Q_HEADS // N_KV_HEADS
WINDOW = 128
ATTN_WIDTH = N_Q_HEADS * HEAD_DIM
KV_WIDTH = N_KV_HEADS * HEAD_DIM
PAIR_WIDTH = 2 * HEAD_DIM
PAIRS_PER_KV = GQA_GROUP // 2
N_BUCKETS = 32
MAX_DISTANCE = 128
SGU_WIDTH = D_MODEL
SGU_GROUPS = 8
SGU_GROUP_DIM = SGU_WIDTH // SGU_GROUPS
CHUNK = 128
EPS = 1e-6
NEG_INF = -1e30
LOG2_E = float(np.log2(np.e))
BF16_SUBLANES = 16
CAST_ROWS = BF16_SUBLANES
STAGE_ROWS = 64

_SPLITS = (ATTN_WIDTH, KV_WIDTH, KV_WIDTH, ATTN_WIDTH, SGU_WIDTH, SGU_WIDTH, SGU_WIDTH, D_MODEL, D_MODEL)
_OFFS = tuple(int(o) for o in np.cumsum((0,) + _SPLITS))
IN_COLS = _OFFS[-1]
(Q_OFF, K_OFF, V_OFF, GA_OFF, U_OFF, VB_OFF, GB_OFF, MA_OFF, MB_OFF) = _OFFS[:-1]

TILE = 256
BLOCKS_PER_TILE = TILE // WINDOW
V7X_VMEM_LIMIT_BYTES = 56 * 1024 * 1024


def _t5_bucket_table():
    qi = np.arange(WINDOW)[:, None]
    kj = np.arange(2 * WINDOW)[None, :]
    dist = np.clip(qi + WINDOW - kj, 0, None)
    max_exact = N_BUCKETS // 2
    large = max_exact + (np.log(np.maximum(dist, max_exact) / max_exact)
                         / np.log(MAX_DISTANCE / max_exact)
                         * (N_BUCKETS - max_exact)).astype(np.int32)
    large = np.minimum(large, N_BUCKETS - 1)
    return np.where(dist < max_exact, dist, large).astype(np.int32)


def _dot(a, b):
    return jnp.dot(a, b, preferred_element_type=jnp.float32)


def _gelu_exact(x):
    return 0.5 * x * (1.0 + lax.erf(x * np.float32(np.sqrt(0.5))))


def _sigmoid(x):
    return 0.5 * (1.0 + jnp.tanh(0.5 * x))


def _silu(x):
    half = 0.5 * x
    return half * (1.0 + jnp.tanh(half))


def _load_cast(src_hbm, layer, dst_ref, stage_ref, sems):
    n_rows, n_cols = dst_ref.shape
    n_chunks = n_rows // STAGE_ROWS

    def copy(chunk, slot):
        return pltpu.make_async_copy(
            src_hbm.at[layer, pl.ds(pl.multiple_of(chunk * STAGE_ROWS, STAGE_ROWS), STAGE_ROWS), :],
            stage_ref.at[slot, :, pl.ds(0, n_cols)], sems.at[slot])

    copy(0, 0).start()

    def body(chunk, carry):
        slot = chunk % 2

        @pl.when(chunk + 1 < n_chunks)
        def _():
            copy(chunk + 1, 1 - slot).start()

        copy(chunk, slot).wait()
        rows = pl.ds(pl.multiple_of(chunk * STAGE_ROWS, STAGE_ROWS), STAGE_ROWS)
        dst_ref[rows, :] = stage_ref[slot, :, 0:n_cols].astype(jnp.bfloat16)
        return carry

    lax.fori_loop(0, n_chunks, body, 0)


def _layer_kernel(rel_bias_ref, sinks_ref, *refs, tiles_per_seq, layer, n_cast, manual_weights):
    (x_ref, xprev_ref, bucket_ref, pre_ref, post_ref, lng_ref, lnb_ref,
     w_in_ref, w_sp_ref, b_sp_ref, w_a_ref, w_b_ref, w_o_ref) = refs[:13]
    cast_src = refs[13:13 + n_cast]
    o_ref = refs[13 + n_cast]
    cast_dst = refs[14 + n_cast:14 + 2 * n_cast]
    (bias_ref, wtril_ref, kvar_ref, vvar_ref, merged_ref,
     h_ref, q_ref, ga_ref, yain_ref, vn_ref, gu_ref, t_ref, ybin_ref, sb_ref, sa_ref,
     tb_ref) = refs[14 + 2 * n_cast:30 + 2 * n_cast]
    step = pl.program_id(0)
    t_idx = jnp.minimum(step, pl.num_programs(0) - 2) % tiles_per_seq
    bf16 = jnp.bfloat16

    if manual_weights:
        hbm_weights = (w_in_ref, w_a_ref, w_b_ref, w_o_ref)
        w_in_ref, w_a_ref, w_b_ref, w_o_ref, stage_ref, load_sems = refs[30 + 2 * n_cast:]

        @pl.when(step == 0)
        def _():
            for src, dst in zip(hbm_weights, (w_in_ref, w_a_ref, w_b_ref, w_o_ref)):
                _load_cast(src, layer, dst, stage_ref, load_sems)

    for src, dst in zip(cast_src, cast_dst):
        dst[...] = src[...].astype(bf16)

    @pl.when(step == 0)
    def _():
        merged_ref[...] = jnp.zeros_like(merged_ref)
        bucket = bucket_ref[...]
        sink_col = lax.broadcasted_iota(jnp.int32, (WINDOW, 2 * WINDOW), 1) == 0
        for h in range(N_Q_HEADS):
            acc = jnp.zeros((WINDOW, 2 * WINDOW), jnp.float32)
            for bk in range(N_BUCKETS):
                acc = jnp.where(bucket == bk, rel_bias_ref[bk, h], acc)
            bias_ref[h] = jnp.where(sink_col, sinks_ref[layer, h], acc) * LOG2_E
        row = lax.broadcasted_iota(jnp.int32, (CHUNK, CHUNK), 0)
        col = lax.broadcasted_iota(jnp.int32, (CHUNK, CHUNK), 1)
        for g in range(SGU_GROUPS):
            wtril_ref[g] = jnp.where(col <= row, w_sp_ref[g], 0.0).astype(bf16)

    @pl.when(t_idx == 0)
    def _():
        for i in range(2 * N_KV_HEADS):
            kvar_ref[i, 0:WINDOW, :] = jnp.zeros((WINDOW, PAIR_WIDTH), bf16)
            vvar_ref[i, 0:WINDOW, :] = jnp.zeros((WINDOW, PAIR_WIDTH), bf16)

    out = _dot(merged_ref[...], w_o_ref[...])
    o_ref[0] = (xprev_ref[0]
                + out * lax.rsqrt(jnp.mean(out * out, axis=-1, keepdims=True) + EPS) * post_ref[...])

    x = x_ref[0]
    h_ref[...] = (x * lax.rsqrt(jnp.mean(x * x, axis=-1, keepdims=True) + EPS) * pre_ref[...]).astype(bf16)

    def proj(off, width):
        return _dot(h_ref[...], w_in_ref[:, off:off + width])

    q_ref[...] = (proj(Q_OFF, ATTN_WIDTH) * np.float32(HEAD_DIM ** -0.5 * LOG2_E)).astype(bf16)
    w_kv = w_in_ref[:, K_OFF:K_OFF + 2 * KV_WIDTH]
    kv = jnp.concatenate([_dot(h_ref[:TILE // 2, :], w_kv), _dot(h_ref[TILE // 2:, :], w_kv)], axis=0)
    low = lax.broadcasted_iota(jnp.int32, (TILE, PAIR_WIDTH), 1) < HEAD_DIM

    def place(var_ref, t):
        zero = jnp.zeros_like(t)
        swapped = pltpu.roll(t, HEAD_DIM, 1)
        var_ref[0, WINDOW:, :] = jnp.where(low, t, zero)
        var_ref[1, WINDOW:, :] = jnp.where(low, zero, swapped)
        var_ref[2, WINDOW:, :] = jnp.where(low, swapped, zero)
        var_ref[3, WINDOW:, :] = jnp.where(low, zero, t)

    place(kvar_ref, kv[:, :KV_WIDTH].astype(bf16))
    place(vvar_ref, kv[:, KV_WIDTH:].astype(bf16))

    qi = lax.broadcasted_iota(jnp.int32, (WINDOW, 2 * WINDOW), 0)
    kj = lax.broadcasted_iota(jnp.int32, (WINDOW, 2 * WINDOW), 1)
    dist = qi + WINDOW - kj
    in_window = jnp.logical_or(jnp.logical_and(dist >= 0, dist < WINDOW), kj == 0)
    first_mask = jnp.logical_and(in_window, jnp.logical_or(jnp.logical_or(kj >= WINDOW, kj == 0), t_idx > 0))
    band_low = lax.broadcasted_iota(jnp.int32, (2 * WINDOW, PAIR_WIDTH), 1) < HEAD_DIM
    ones_even = jnp.where(band_low, 1.0, 0.0).astype(bf16)
    ones_odd = jnp.where(band_low, 0.0, 1.0).astype(bf16)
    first_row = lax.broadcasted_iota(jnp.int32, (BF16_SUBLANES, PAIR_WIDTH), 0) == 0

    def band(var_ref, variant, r0):
        head = var_ref[variant, r0:r0 + BF16_SUBLANES, :]
        head = jnp.where(first_row, jnp.zeros_like(head), head)
        return jnp.concatenate([head, var_ref[variant, r0 + BF16_SUBLANES:r0 + 2 * WINDOW, :]], axis=0)

    def attn_scores(blk, j):
        r0 = blk * WINDOW
        q_g = jnp.concatenate(
            [q_ref[r0:r0 + WINDOW, (PAIRS_PER_KV * j + p) * PAIR_WIDTH:(PAIRS_PER_KV * j + p + 1) * PAIR_WIDTH]
             for p in range(PAIRS_PER_KV)], axis=0)
        k_g = jnp.concatenate([band(kvar_ref, 2 * j + e, r0) for e in range(2)], axis=0)
        return lax.dot_general(q_g, k_g, (((1,), (1,)), ((), ())),
                               preferred_element_type=jnp.float32)

    def attn_softmax(blk, j, s_g):
        mask = first_mask if blk == 0 else in_window
        rows = []
        for p in range(PAIRS_PER_KV):
            probs = []
            for e in range(2):
                hq = GQA_GROUP * j + 2 * p + e
                s = s_g[p * WINDOW:(p + 1) * WINDOW, e * 2 * WINDOW:(e + 1) * 2 * WINDOW]
                s = jnp.where(mask, s + bias_ref[hq], NEG_INF)
                m = jnp.max(s, axis=-1, keepdims=True)
                probs.append(jnp.exp2(s - m).astype(bf16))
            rows.append(jnp.concatenate(probs, axis=1))
        return jnp.concatenate(rows, axis=0)

    def attn_pv(blk, j, p_g):
        r0 = blk * WINDOW
        v_g = jnp.concatenate(
            [jnp.concatenate([band(vvar_ref, 2 * j + e, r0), ones], axis=1)
             for e, ones in ((0, ones_even), (1, ones_odd))], axis=0)
        r_g = _dot(p_g, v_g)
        for p in range(PAIRS_PER_KV):
            r = r_g[p * WINDOW:(p + 1) * WINDOW]
            c0 = (PAIRS_PER_KV * j + p) * PAIR_WIDTH
            o_pair = r[:, :PAIR_WIDTH] / r[:, PAIR_WIDTH:]
            yain_ref[r0:r0 + WINDOW, c0:c0 + PAIR_WIDTH] = (
                o_pair * ga_ref[r0:r0 + WINDOW, c0:c0 + PAIR_WIDTH]).astype(bf16)

    def spatial_mix():
        n_chunks = TILE // CHUNK
        for g in range(SGU_GROUPS):
            c_lo = g * SGU_GROUP_DIM
            v_g = jnp.concatenate(
                [vn_ref[c * CHUNK:(c + 1) * CHUNK, c_lo:c_lo + SGU_GROUP_DIM] for c in range(n_chunks)], axis=1)
            m_g = _dot(wtril_ref[g], v_g) + b_sp_ref[g]
            for c in range(n_chunks):
                rows = slice(c * CHUNK, (c + 1) * CHUNK)
                cols = slice(c_lo, c_lo + SGU_GROUP_DIM)
                ybin_ref[rows, cols] = (m_g[:, c * SGU_GROUP_DIM:(c + 1) * SGU_GROUP_DIM]
                                        * t_ref[rows, cols]).astype(bf16)

    def job_v():
        v_act = _gelu_exact(proj(VB_OFF, SGU_WIDTH))
        mu = jnp.mean(v_act, axis=-1, keepdims=True)
        vc = v_act - mu
        vn_ref[...] = (vc * lax.rsqrt(jnp.mean(vc * vc, axis=-1, keepdims=True) + EPS) * lng_ref[...]
                       + lnb_ref[...]).astype(bf16)

    def job_u():
        gu_ref[...] = _gelu_exact(proj(U_OFF, SGU_WIDTH)).astype(bf16)

    def job_gate_b():
        t_ref[...] = (_silu(proj(GB_OFF, SGU_WIDTH)) * gu_ref[...]).astype(bf16)

    def job_merge_b():
        sb_ref[...] = _sigmoid(proj(MB_OFF, D_MODEL)).astype(bf16)

    def job_branch_b():
        tb_ref[...] = _dot(ybin_ref[...], w_b_ref[...]) * sb_ref[...]

    def job_merge_a():
        sa_ref[...] = _sigmoid(proj(MA_OFF, D_MODEL)).astype(bf16)

    dense_jobs = [job_v, job_u, job_gate_b, spatial_mix, job_merge_b, job_branch_b, job_merge_a]
    groups = [(blk, j) for blk in range(BLOCKS_PER_TILE) for j in range(N_KV_HEADS)]
    ga_ref[...] = _silu(proj(GA_OFF, ATTN_WIDTH)).astype(bf16)
    s_g = attn_scores(*groups[0])
    for gi, group in enumerate(groups):
        if dense_jobs:
            dense_jobs.pop(0)()
        p_g = attn_softmax(*group, s_g)
        if gi + 1 < len(groups):
            s_g = attn_scores(*groups[gi + 1])
        attn_pv(*group, p_g)

    for i in range(2 * N_KV_HEADS):
        kvar_ref[i, 0:WINDOW, :] = kvar_ref[i, TILE:TILE + WINDOW, :]
        vvar_ref[i, 0:WINDOW, :] = vvar_ref[i, TILE:TILE + WINDOW, :]

    for job in dense_jobs:
        job()
    merged_ref[...] = (_dot(yain_ref[...], w_a_ref[...]) * sa_ref[...] + tb_ref[...]).astype(bf16)


def _resident(shape):
    return pl.BlockSpec(shape, lambda s, *_: (0,) * len(shape), pipeline_mode=pl.Buffered(1))


def _layer_slice(layer, shape):
    return pl.BlockSpec((None,) + shape, lambda s, *_: (layer,) + (0,) * len(shape),
                        pipeline_mode=pl.Buffered(1))


def _layer(layer, x, rel_bias, sinks, bucket, pre, post, lng, lnb, w_sp, b_sp, weights, next_weights):
    B, S, D = x.shape
    w_in, w_a, w_b, w_o = weights
    manual_weights = w_in.dtype == jnp.float32
    assert D == D_MODEL and S % TILE == 0 and w_in.shape[1:] == (D_MODEL, IN_COLS)
    tiles_per_seq = S // TILE
    n_tiles = B * tiles_per_seq
    cast_blocks = D_MODEL // CAST_ROWS
    assert n_tiles >= 2 * cast_blocks or not next_weights

    def tile_index(tile):
        return (tile // tiles_per_seq, tile % tiles_per_seq, 0)

    def cast_block(s):
        return jnp.minimum(s // (n_tiles // cast_blocks), cast_blocks - 1)

    x_spec = pl.BlockSpec((1, TILE, D), lambda s, *_: tile_index(jnp.minimum(s, n_tiles - 1)))
    o_spec = pl.BlockSpec((1, TILE, D), lambda s, *_: tile_index(jnp.maximum(s - 1, 0)))
    cast_in_specs = [pl.BlockSpec((None, CAST_ROWS, w.shape[2]), lambda s, *_: (layer + 1, cast_block(s), 0))
                     for w in next_weights]
    cast_out_specs = [pl.BlockSpec((None, CAST_ROWS, w.shape[2]), lambda s, *_: (0, cast_block(s), 0))
                      for w in next_weights]
    cast_out_shapes = [jax.ShapeDtypeStruct((1,) + w.shape[1:], jnp.bfloat16) for w in next_weights]
    stage_bf16 = pltpu.VMEM((TILE, D), jnp.bfloat16)
    stage_f32 = pltpu.VMEM((TILE, D), jnp.float32)
    weight_shapes = ((D, IN_COLS), (ATTN_WIDTH, D), (SGU_WIDTH, D), (D, D))
    if manual_weights:
        weight_specs = [pl.BlockSpec(memory_space=pl.ANY)] * 4
        weight_scratch = ([pltpu.VMEM(shape, jnp.bfloat16) for shape in weight_shapes]
                          + [pltpu.VMEM((2, STAGE_ROWS, IN_COLS), jnp.float32), pltpu.SemaphoreType.DMA((2,))])
    else:
        weight_specs = [_layer_slice(0, shape) for shape in weight_shapes]
        weight_scratch = []
    grid_spec = pltpu.PrefetchScalarGridSpec(
        num_scalar_prefetch=2,
        grid=(n_tiles + 1,),
        in_specs=[
            x_spec,
            o_spec,
            _resident((WINDOW, 2 * WINDOW)),
            _layer_slice(layer, (1, D)), _layer_slice(layer, (1, D)),
            _layer_slice(layer, (1, SGU_WIDTH)), _layer_slice(layer, (1, SGU_WIDTH)),
            weight_specs[0],
            _layer_slice(layer, (SGU_GROUPS, CHUNK, CHUNK)),
            _layer_slice(layer, (SGU_GROUPS, CHUNK, 1)),
            weight_specs[1], weight_specs[2], weight_specs[3],
        ] + cast_in_specs,
        out_specs=[o_spec] + cast_out_specs,
        scratch_shapes=[
            pltpu.VMEM((N_Q_HEADS, WINDOW, 2 * WINDOW), jnp.float32),
            pltpu.VMEM((SGU_GROUPS, CHUNK, CHUNK), jnp.bfloat16),
            pltpu.VMEM((2 * N_KV_HEADS, TILE + WINDOW, PAIR_WIDTH), jnp.bfloat16),
            pltpu.VMEM((2 * N_KV_HEADS, TILE + WINDOW, PAIR_WIDTH), jnp.bfloat16),
            stage_bf16,
            stage_bf16, stage_bf16,
            stage_bf16, stage_bf16,
            stage_bf16, stage_bf16, stage_bf16,
            stage_bf16, stage_bf16, stage_bf16,
            stage_f32,
        ] + weight_scratch,
    )
    outs = pl.pallas_call(
        functools.partial(_layer_kernel, tiles_per_seq=tiles_per_seq, layer=layer, n_cast=len(next_weights),
                          manual_weights=manual_weights),
        out_shape=[jax.ShapeDtypeStruct(x.shape, x.dtype)] + cast_out_shapes,
        grid_spec=grid_spec,
        compiler_params=pltpu.CompilerParams(
            dimension_semantics=("arbitrary",),
            vmem_limit_bytes=V7X_VMEM_LIMIT_BYTES),
        name="hybrid_layer",
    )(rel_bias, sinks, x, x, bucket, pre, post, lng, lnb, w_in, w_sp, b_sp, w_a, w_b, w_o, *next_weights)
    return outs[0], tuple(outs[1:])


def kernel(x, rel_bias, pre_norm, post_norm, w_in, sinks, sgu_ln_g, sgu_ln_b, w_spatial, b_spatial,
           w_branch_a, w_branch_b, w_out):
    depth = w_in.shape[0]
    bucket = jnp.asarray(_t5_bucket_table())
    small = (rel_bias, sinks, bucket,
             pre_norm[:, None], post_norm[:, None], sgu_ln_g[:, None], sgu_ln_b[:, None],
             w_spatial, b_spatial[..., None])
    stacked = (w_in, w_branch_a, w_branch_b, w_out)
    weights = stacked
    for l in range(depth):
        x, weights = _layer(l, x, *small, weights, stacked if l + 1 < depth else ())
    return x
```

```python
import functools

import jax
import jax.numpy as jnp
import numpy as np
from jax import lax
from jax.experimental import pallas as pl
from jax.experimental.pallas import tpu as pltpu

D_MODEL = 1024
HEAD_DIM = 64
N_Q_HEADS = 16
N_KV_HEADS = 2
GQA_GROUP = N_Q_HEADS // N_KV_HEADS
WINDOW = 128
ATTN_WIDTH = N_Q_HEADS * HEAD_DIM
KV_WIDTH = N_KV_HEADS * HEAD_DIM
PAIR_WIDTH = 2 * HEAD_DIM
PAIRS_PER_KV = GQA_GROUP // 2
N_BUCKETS = 32
MAX_DISTANCE = 128
SGU_WIDTH = D_MODEL
SGU_GROUPS = 8
SGU_GROUP_DIM = SGU_WIDTH // SGU_GROUPS
CHUNK = 128
EPS = 1e-6
NEG_INF = -1e30
LOG2_E = float(np.log2(np.e))
BF16_SUBLANES = 16
CAST_ROWS = BF16_SUBLANES
STAGE_ROWS = 64
STAGE_SLOTS = 4

_SPLITS = (ATTN_WIDTH, KV_WIDTH, KV_WIDTH, ATTN_WIDTH, SGU_WIDTH, SGU_WIDTH, SGU_WIDTH, D_MODEL, D_MODEL)
_OFFS = tuple(int(o) for o in np.cumsum((0,) + _SPLITS))
IN_COLS = _OFFS[-1]
(Q_OFF, K_OFF, V_OFF, GA_OFF, U_OFF, VB_OFF, GB_OFF, MA_OFF, MB_OFF) = _OFFS[:-1]

TILE = 256
BLOCKS_PER_TILE = TILE // WINDOW
V7X_VMEM_LIMIT_BYTES = 56 * 1024 * 1024


def _t5_bucket_table():
    qi = np.arange(WINDOW)[:, None]
    kj = np.arange(2 * WINDOW)[None, :]
    dist = np.clip(qi + WINDOW - kj, 0, None)
    max_exact = N_BUCKETS // 2
    large = max_exact + (np.log(np.maximum(dist, max_exact) / max_exact)
                         / np.log(MAX_DISTANCE / max_exact)
                         * (N_BUCKETS - max_exact)).astype(np.int32)
    large = np.minimum(large, N_BUCKETS - 1)
    return np.where(dist < max_exact, dist, large).astype(np.int32)


def _dot(a, b):
    return jnp.dot(a, b, preferred_element_type=jnp.float32)


def _gelu_exact(x):
    return 0.5 * x * (1.0 + lax.erf(x * np.float32(np.sqrt(0.5))))


def _sigmoid(x):
    return 0.5 * (1.0 + jnp.tanh(0.5 * x))


def _silu(x):
    half = 0.5 * x
    return half * (1.0 + jnp.tanh(half))


def _load_cast(src_hbm, layer, dst_ref, stage_ref, sems):
    n_rows, n_cols = dst_ref.shape
    n_chunks = n_rows // STAGE_ROWS

    def copy(chunk, slot):
        return pltpu.make_async_copy(
            src_hbm.at[layer, pl.ds(pl.multiple_of(chunk * STAGE_ROWS, STAGE_ROWS), STAGE_ROWS), :],
            stage_ref.at[slot, :, pl.ds(0, n_cols)], sems.at[slot])

    for chunk in range(STAGE_SLOTS - 1):
        copy(chunk, chunk).start()

    def body(chunk, carry):
        slot = chunk % STAGE_SLOTS
        ahead = chunk + STAGE_SLOTS - 1

        @pl.when(ahead < n_chunks)
        def _():
            copy(ahead, ahead % STAGE_SLOTS).start()

        copy(chunk, slot).wait()
        rows = pl.ds(pl.multiple_of(chunk * STAGE_ROWS, STAGE_ROWS), STAGE_ROWS)
        dst_ref[rows, :] = stage_ref[slot, :, 0:n_cols].astype(jnp.bfloat16)
        return carry

    lax.fori_loop(0, n_chunks, body, 0)


def _layer_kernel(rel_bias_ref, sinks_ref, *refs, tiles_per_seq, layer, n_cast, manual_weights):
    (x_ref, xprev_ref, bucket_ref, pre_ref, post_ref, lng_ref, lnb_ref,
     w_in_ref, w_sp_ref, b_sp_ref, w_a_ref, w_b_ref, w_o_ref) = refs[:13]
    cast_src = refs[13:13 + n_cast]
    o_ref = refs[13 + n_cast]
    cast_dst = refs[14 + n_cast:14 + 2 * n_cast]
    (bias_ref, wtril_ref, kvar_ref, vvar_ref, merged_ref,
     h_ref, q_ref, ga_ref, yain_ref, vn_ref, gu_ref, t_ref, ybin_ref, sb_ref, sa_ref,
     tb_ref) = refs[14 + 2 * n_cast:30 + 2 * n_cast]
    step = pl.program_id(0)
    t_idx = jnp.minimum(step, pl.num_programs(0) - 2) % tiles_per_seq
    bf16 = jnp.bfloat16

    if manual_weights:
        hbm_weights = (w_in_ref, w_a_ref, w_b_ref, w_o_ref)
        w_in_ref, w_a_ref, w_b_ref, w_o_ref, stage_ref, load_sems = refs[30 + 2 * n_cast:]

        @pl.when(step == 0)
        def _():
            for src, dst in zip(hbm_weights, (w_in_ref, w_a_ref, w_b_ref, w_o_ref)):
                _load_cast(src, layer, dst, stage_ref, load_sems)

    for src, dst in zip(cast_src, cast_dst):
        dst[...] = src[...].astype(bf16)

    @pl.when(step == 0)
    def _():
        merged_ref[...] = jnp.zeros_like(merged_ref)
        bucket = bucket_ref[...]
        sink_col = lax.broadcasted_iota(jnp.int32, (WINDOW, 2 * WINDOW), 1) == 0
        for h in range(N_Q_HEADS):
            acc = jnp.zeros((WINDOW, 2 * WINDOW), jnp.float32)
            for bk in range(N_BUCKETS):
                acc = jnp.where(bucket == bk, rel_bias_ref[bk, h], acc)
            bias_ref[h] = jnp.where(sink_col, sinks_ref[layer, h], acc) * LOG2_E
        row = lax.broadcasted_iota(jnp.int32, (CHUNK, CHUNK), 0)
        col = lax.broadcasted_iota(jnp.int32, (CHUNK, CHUNK), 1)
        for g in range(SGU_GROUPS):
            wtril_ref[g] = jnp.where(col <= row, w_sp_ref[g], 0.0).astype(bf16)

    @pl.when(t_idx == 0)
    def _():
        for i in range(2 * N_KV_HEADS):
            kvar_ref[i, 0:WINDOW, :] = jnp.zeros((WINDOW, PAIR_WIDTH), bf16)
            vvar_ref[i, 0:WINDOW, :] = jnp.zeros((WINDOW, PAIR_WIDTH), bf16)

    out = _dot(merged_ref[...], w_o_ref[...])
    o_ref[0] = (xprev_ref[0]
                + out * lax.rsqrt(jnp.mean(out * out, axis=-1, keepdims=True) + EPS) * post_ref[...])

    x = x_ref[0]
    h_ref[...] = (x * lax.rsqrt(jnp.mean(x * x, axis=-1, keepdims=True) + EPS) * pre_ref[...]).astype(bf16)

    def proj(off, width):
        return _dot(h_ref[...], w_in_ref[:, off:off + width])

    q_ref[...] = (proj(Q_OFF, ATTN_WIDTH) * np.float32(HEAD_DIM ** -0.5 * LOG2_E)).astype(bf16)
    w_kv = w_in_ref[:, K_OFF:K_OFF + 2 * KV_WIDTH]
    kv = jnp.concatenate([_dot(h_ref[:TILE // 2, :], w_kv), _dot(h_ref[TILE // 2:, :], w_kv)], axis=0)
    low = lax.broadcasted_iota(jnp.int32, (TILE, PAIR_WIDTH), 1) < HEAD_DIM

    def place(var_ref, t):
        zero = jnp.zeros_like(t)
        swapped = pltpu.roll(t, HEAD_DIM, 1)
        var_ref[0, WINDOW:, :] = jnp.where(low, t, zero)
        var_ref[1, WINDOW:, :] = jnp.where(low, zero, swapped)
        var_ref[2, WINDOW:, :] = jnp.where(low, swapped, zero)
        var_ref[3, WINDOW:, :] = jnp.where(low, zero, t)

    place(kvar_ref, kv[:, :KV_WIDTH].astype(bf16))
    place(vvar_ref, kv[:, KV_WIDTH:].astype(bf16))

    qi = lax.broadcasted_iota(jnp.int32, (WINDOW, 2 * WINDOW), 0)
    kj = lax.broadcasted_iota(jnp.int32, (WINDOW, 2 * WINDOW), 1)
    dist = qi + WINDOW - kj
    in_window = jnp.logical_or(jnp.logical_and(dist >= 0, dist < WINDOW), kj == 0)
    first_mask = jnp.logical_and(in_window, jnp.logical_or(jnp.logical_or(kj >= WINDOW, kj == 0), t_idx > 0))
    band_low = lax.broadcasted_iota(jnp.int32, (2 * WINDOW, PAIR_WIDTH), 1) < HEAD_DIM
    ones_even = jnp.where(band_low, 1.0, 0.0).astype(bf16)
    ones_odd = jnp.where(band_low, 0.0, 1.0).astype(bf16)
    first_row = lax.broadcasted_iota(jnp.int32, (BF16_SUBLANES, PAIR_WIDTH), 0) == 0

    def band(var_ref, variant, r0):
        head = var_ref[variant, r0:r0 + BF16_SUBLANES, :]
        head = jnp.where(first_row, jnp.zeros_like(head), head)
        return jnp.concatenate([head, var_ref[variant, r0 + BF16_SUBLANES:r0 + 2 * WINDOW, :]], axis=0)

    def attn_scores(blk, j):
        r0 = blk * WINDOW
        q_g = jnp.concatenate(
            [q_ref[r0:r0 + WINDOW, (PAIRS_PER_KV * j + p) * PAIR_WIDTH:(PAIRS_PER_KV * j + p + 1) * PAIR_WIDTH]
             for p in range(PAIRS_PER_KV)], axis=0)
        k_g = jnp.concatenate([band(kvar_ref, 2 * j + e, r0) for e in range(2)], axis=0)
        return lax.dot_general(q_g, k_g, (((1,), (1,)), ((), ())),
                               preferred_element_type=jnp.float32)

    def attn_softmax(blk, j, s_g):
        mask = first_mask if blk == 0 else in_window
        rows = []
        for p in range(PAIRS_PER_KV):
            probs = []
            for e in range(2):
                hq = GQA_GROUP * j + 2 * p + e
                s = s_g[p * WINDOW:(p + 1) * WINDOW, e * 2 * WINDOW:(e + 1) * 2 * WINDOW]
                s = jnp.where(mask, s + bias_ref[hq], NEG_INF)
                m = jnp.max(s, axis=-1, keepdims=True)
                probs.append(jnp.exp2(s - m).astype(bf16))
            rows.append(jnp.concatenate(probs, axis=1))
        return jnp.concatenate(rows, axis=0)

    def attn_pv(blk, j, p_g):
        r0 = blk * WINDOW
        v_g = jnp.concatenate(
            [jnp.concatenate([band(vvar_ref, 2 * j + e, r0), ones], axis=1)
             for e, ones in ((0, ones_even), (1, ones_odd))], axis=0)
        r_g = _dot(p_g, v_g)
        for p in range(PAIRS_PER_KV):
            r = r_g[p * WINDOW:(p + 1) * WINDOW]
            c0 = (PAIRS_PER_KV * j + p) * PAIR_WIDTH
            o_pair = r[:, :PAIR_WIDTH] / r[:, PAIR_WIDTH:]
            yain_ref[r0:r0 + WINDOW, c0:c0 + PAIR_WIDTH] = (
                o_pair * ga_ref[r0:r0 + WINDOW, c0:c0 + PAIR_WIDTH]).astype(bf16)

    def spatial_mix():
        n_chunks = TILE // CHUNK
        for g in range(SGU_GROUPS):
            c_lo = g * SGU_GROUP_DIM
            v_g = jnp.concatenate(
                [vn_ref[c * CHUNK:(c + 1) * CHUNK, c_lo:c_lo + SGU_GROUP_DIM] for c in range(n_chunks)], axis=1)
            m_g = _dot(wtril_ref[g], v_g) + b_sp_ref[g]
            for c in range(n_chunks):
                rows = slice(c * CHUNK, (c + 1) * CHUNK)
                cols = slice(c_lo, c_lo + SGU_GROUP_DIM)
                ybin_ref[rows, cols] = (m_g[:, c * SGU_GROUP_DIM:(c + 1) * SGU_GROUP_DIM]
                                        * t_ref[rows, cols]).astype(bf16)

    def job_v():
        v_act = _gelu_exact(proj(VB_OFF, SGU_WIDTH))
        mu = jnp.mean(v_act, axis=-1, keepdims=True)
        vc = v_act - mu
        vn_ref[...] = (vc * lax.rsqrt(jnp.mean(vc * vc, axis=-1, keepdims=True) + EPS) * lng_ref[...]
                       + lnb_ref[...]).astype(bf16)

    def job_u():
        gu_ref[...] = _gelu_exact(proj(U_OFF, SGU_WIDTH)).astype(bf16)

    def job_gate_b():
        t_ref[...] = (_silu(proj(GB_OFF, SGU_WIDTH)) * gu_ref[...]).astype(bf16)

    def job_merge_b():
        sb_ref[...] = _sigmoid(proj(MB_OFF, D_MODEL)).astype(bf16)

    def job_branch_b():
        tb_ref[...] = _dot(ybin_ref[...], w_b_ref[...]) * sb_ref[...]

    def job_merge_a():
        sa_ref[...] = _sigmoid(proj(MA_OFF, D_MODEL)).astype(bf16)

    dense_jobs = [job_v, job_u, job_gate_b, spatial_mix, job_merge_b, job_branch_b, job_merge_a]
    groups = [(blk, j) for blk in range(BLOCKS_PER_TILE) for j in range(N_KV_HEADS)]
    ga_ref[...] = _silu(proj(GA_OFF, ATTN_WIDTH)).astype(bf16)
    s_g = attn_scores(*groups[0])
    for gi, group in enumerate(groups):
        if dense_jobs:
            dense_jobs.pop(0)()
        p_g = attn_softmax(*group, s_g)
        if gi + 1 < len(groups):
            s_g = attn_scores(*groups[gi + 1])
        attn_pv(*group, p_g)

    for i in range(2 * N_KV_HEADS):
        kvar_ref[i, 0:WINDOW, :] = kvar_ref[i, TILE:TILE + WINDOW, :]
        vvar_ref[i, 0:WINDOW, :] = vvar_ref[i, TILE:TILE + WINDOW, :]

    for job in dense_jobs:
        job()
    merged_ref[...] = (_dot(yain_ref[...], w_a_ref[...]) * sa_ref[...] + tb_ref[...]).astype(bf16)


def _resident(shape):
    return pl.BlockSpec(shape, lambda s, *_: (0,) * len(shape), pipeline_mode=pl.Buffered(1))


def _layer_slice(layer, shape):
    return pl.BlockSpec((None,) + shape, lambda s, *_: (layer,) + (0,) * len(shape),
                        pipeline_mode=pl.Buffered(1))


def _layer(layer, x, rel_bias, sinks, bucket, pre, post, lng, lnb, w_sp, b_sp, weights, next_weights):
    B, S, D = x.shape
    w_in, w_a, w_b, w_o = weights
    manual_weights = w_in.dtype == jnp.float32
    assert D == D_MODEL and S % TILE == 0 and w_in.shape[1:] == (D_MODEL, IN_COLS)
    tiles_per_seq = S // TILE
    n_tiles = B * tiles_per_seq
    cast_blocks = D_MODEL // CAST_ROWS
    assert n_tiles >= 2 * cast_blocks or not next_weights

    def tile_index(tile):
        return (tile // tiles_per_seq, tile % tiles_per_seq, 0)

    def cast_block(s):
        return jnp.minimum(s // (n_tiles // cast_blocks), cast_blocks - 1)

    x_spec = pl.BlockSpec((1, TILE, D), lambda s, *_: tile_index(jnp.minimum(s, n_tiles - 1)))
    o_spec = pl.BlockSpec((1, TILE, D), lambda s, *_: tile_index(jnp.maximum(s - 1, 0)))
    cast_in_specs = [pl.BlockSpec((None, CAST_ROWS, w.shape[2]), lambda s, *_: (layer + 1, cast_block(s), 0))
                     for w in next_weights]
    cast_out_specs = [pl.BlockSpec((None, CAST_ROWS, w.shape[2]), lambda s, *_: (0, cast_block(s), 0))
                      for w in next_weights]
    cast_out_shapes = [jax.ShapeDtypeStruct((1,) + w.shape[1:], jnp.bfloat16) for w in next_weights]
    stage_bf16 = pltpu.VMEM((TILE, D), jnp.bfloat16)
    stage_f32 = pltpu.VMEM((TILE, D), jnp.float32)
    weight_shapes = ((D, IN_COLS), (ATTN_WIDTH, D), (SGU_WIDTH, D), (D, D))
    if manual_weights:
        weight_specs = [pl.BlockSpec(memory_space=pl.ANY)] * 4
        weight_scratch = ([pltpu.VMEM(shape, jnp.bfloat16) for shape in weight_shapes]
                          + [pltpu.VMEM((STAGE_SLOTS, STAGE_ROWS, IN_COLS), jnp.float32),
                             pltpu.SemaphoreType.DMA((STAGE_SLOTS,))])
    else:
        weight_specs = [_layer_slice(0, shape) for shape in weight_shapes]
        weight_scratch = []
    grid_spec = pltpu.PrefetchScalarGridSpec(
        num_scalar_prefetch=2,
        grid=(n_tiles + 1,),
        in_specs=[
            x_spec,
            o_spec,
            _resident((WINDOW, 2 * WINDOW)),
            _layer_slice(layer, (1, D)), _layer_slice(layer, (1, D)),
            _layer_slice(layer, (1, SGU_WIDTH)), _layer_slice(layer, (1, SGU_WIDTH)),
            weight_specs[0],
            _layer_slice(layer, (SGU_GROUPS, CHUNK, CHUNK)),
            _layer_slice(layer, (SGU_GROUPS, CHUNK, 1)),
            weight_specs[1], weight_specs[2], weight_specs[3],
        ] + cast_in_specs,
        out_specs=[o_spec] + cast_out_specs,
        scratch_shapes=[
            pltpu.VMEM((N_Q_HEADS, WINDOW, 2 * WINDOW), jnp.float32),
            pltpu.VMEM((SGU_GROUPS, CHUNK, CHUNK), jnp.bfloat16),
            pltpu.VMEM((2 * N_KV_HEADS, TILE + WINDOW, PAIR_WIDTH), jnp.bfloat16),
            pltpu.VMEM((2 * N_KV_HEADS, TILE + WINDOW, PAIR_WIDTH), jnp.bfloat16),
            stage_bf16,
            stage_bf16, stage_bf16,
            stage_bf16, stage_bf16,
            stage_bf16, stage_bf16, stage_bf16,
            stage_bf16, stage_bf16, stage_bf16,
            stage_f32,
        ] + weight_scratch,
    )
    outs = pl.pallas_call(
        functools.partial(_layer_kernel, tiles_per_seq=tiles_per_seq, layer=layer, n_cast=len(next_weights),
                          manual_weights=manual_weights),
        out_shape=[jax.ShapeDtypeStruct(x.shape, x.dtype)] + cast_out_shapes,
        grid_spec=grid_spec,
        compiler_params=pltpu.CompilerParams(
            dimension_semantics=("arbitrary",),
            vmem_limit_bytes=V7X_VMEM_LIMIT_BYTES),
        name="hybrid_layer",
    )(rel_bias, sinks, x, x, bucket, pre, post, lng, lnb, w_in, w_sp, b_sp, w_a, w_b, w_o, *next_weights)
    return outs[0], tuple(outs[1:])


def kernel(x, rel_bias, pre_norm, post_norm, w_in, sinks, sgu_ln_g, sgu_ln_b, w_spatial, b_spatial,
           w_branch_a, w_branch_b, w_out):
    depth = w_in.shape[0]
    bucket = jnp.asarray(_t5_bucket_table())
    small = (rel_bias, sinks, bucket,
             pre_norm[:, None], post_norm[:, None], sgu_ln_g[:, None], sgu_ln_b[:, None],
             w_spatial, b_spatial[..., None])
    stacked = (w_in, w_branch_a, w_branch_b, w_out)
    weights = stacked
    for l in range(depth):
        x, weights = _layer(l, x, *small, weights, stacked if l + 1 < depth else ())
    return x
```

```python
import functools

import jax
import jax.numpy as jnp
import numpy as np
from jax import lax
from jax.experimental import pallas as pl
from jax.experimental.pallas import tpu as pltpu

D_MODEL = 1024
HEAD_DIM = 64
N_Q_HEADS = 16
N_KV_HEADS = 2
GQA_GROUP = N_Q_HEADS // N_KV_HEADS
WINDOW = 128
ATTN_WIDTH = N_Q_HEADS * HEAD_DIM
KV_WIDTH = N_KV_HEADS * HEAD_DIM
PAIR_WIDTH = 2 * HEAD_DIM
PAIRS_PER_KV = GQA_GROUP // 2
N_BUCKETS = 32
MAX_DISTANCE = 128
SGU_WIDTH = D_MODEL
SGU_GROUPS = 8
SGU_GROUP_DIM = SGU_WIDTH // SGU_GROUPS
CHUNK = 128
EPS = 1e-6
NEG_INF = -1e30
LOG2_E = float(np.log2(np.e))
BF16_SUBLANES = 16
CAST_ROWS = BF16_SUBLANES
STAGE_ROWS = 64
STAGE_ROWS_SQUARE = 256
STAGE_SLOTS = 4

_SPLITS = (ATTN_WIDTH, KV_WIDTH, KV_WIDTH, ATTN_WIDTH, SGU_WIDTH, SGU_WIDTH, SGU_WIDTH, D_MODEL, D_MODEL)
_OFFS = tuple(int(o) for o in np.cumsum((0,) + _SPLITS))
IN_COLS = _OFFS[-1]
(Q_OFF, K_OFF, V_OFF, GA_OFF, U_OFF, VB_OFF, GB_OFF, MA_OFF, MB_OFF) = _OFFS[:-1]

TILE = 256
BLOCKS_PER_TILE = TILE // WINDOW
V7X_VMEM_LIMIT_BYTES = 56 * 1024 * 1024


def _t5_bucket_table():
    qi = np.arange(WINDOW)[:, None]
    kj = np.arange(2 * WINDOW)[None, :]
    dist = np.clip(qi + WINDOW - kj, 0, None)
    max_exact = N_BUCKETS // 2
    large = max_exact + (np.log(np.maximum(dist, max_exact) / max_exact)
                         / np.log(MAX_DISTANCE / max_exact)
                         * (N_BUCKETS - max_exact)).astype(np.int32)
    large = np.minimum(large, N_BUCKETS - 1)
    return np.where(dist < max_exact, dist, large).astype(np.int32)


def _dot(a, b):
    return jnp.dot(a, b, preferred_element_type=jnp.float32)


def _gelu_exact(x):
    return 0.5 * x * (1.0 + lax.erf(x * np.float32(np.sqrt(0.5))))


def _sigmoid(x):
    return 0.5 * (1.0 + jnp.tanh(0.5 * x))


def _silu(x):
    half = 0.5 * x
    return half * (1.0 + jnp.tanh(half))


def _load_cast(src_hbm, layer, dst_ref, stage_ref, sems):
    n_rows, n_cols = dst_ref.shape
    chunk_rows = stage_ref.shape[1]
    assert stage_ref.shape[2] == n_cols and n_rows % chunk_rows == 0
    n_chunks = n_rows // chunk_rows
    assert n_chunks >= STAGE_SLOTS - 1

    def copy(chunk, slot):
        return pltpu.make_async_copy(
            src_hbm.at[layer, pl.ds(pl.multiple_of(chunk * chunk_rows, chunk_rows), chunk_rows), :],
            stage_ref.at[slot], sems.at[slot])

    for chunk in range(STAGE_SLOTS - 1):
        copy(chunk, chunk).start()

    def body(chunk, carry):
        slot = chunk % STAGE_SLOTS
        ahead = chunk + STAGE_SLOTS - 1

        @pl.when(ahead < n_chunks)
        def _():
            copy(ahead, ahead % STAGE_SLOTS).start()

        copy(chunk, slot).wait()
        rows = pl.ds(pl.multiple_of(chunk * chunk_rows, chunk_rows), chunk_rows)
        dst_ref[rows, :] = stage_ref[slot].astype(jnp.bfloat16)
        return carry

    lax.fori_loop(0, n_chunks, body, 0)


def _layer_kernel(rel_bias_ref, sinks_ref, *refs, tiles_per_seq, layer, n_cast, manual_weights):
    (x_ref, xprev_ref, bucket_ref, pre_ref, post_ref, lng_ref, lnb_ref,
     w_in_ref, w_sp_ref, b_sp_ref, w_a_ref, w_b_ref, w_o_ref) = refs[:13]
    cast_src = refs[13:13 + n_cast]
    o_ref = refs[13 + n_cast]
    cast_dst = refs[14 + n_cast:14 + 2 * n_cast]
    (bias_ref, wtril_ref, kvar_ref, vvar_ref, merged_ref,
     h_ref, q_ref, ga_ref, yain_ref, vn_ref, gu_ref, t_ref, ybin_ref, sb_ref, sa_ref,
     tb_ref) = refs[14 + 2 * n_cast:30 + 2 * n_cast]
    step = pl.program_id(0)
    t_idx = jnp.minimum(step, pl.num_programs(0) - 2) % tiles_per_seq
    bf16 = jnp.bfloat16

    if manual_weights:
        hbm_weights = (w_in_ref, w_a_ref, w_b_ref, w_o_ref)
        w_in_ref, w_a_ref, w_b_ref, w_o_ref, stage_in_ref, stage_sq_ref, load_sems = refs[30 + 2 * n_cast:]

        @pl.when(step == 0)
        def _():
            stages = (stage_in_ref, stage_sq_ref, stage_sq_ref, stage_sq_ref)
            for src, dst, stage in zip(hbm_weights, (w_in_ref, w_a_ref, w_b_ref, w_o_ref), stages):
                _load_cast(src, layer, dst, stage, load_sems)

    for src, dst in zip(cast_src, cast_dst):
        dst[...] = src[...].astype(bf16)

    @pl.when(step == 0)
    def _():
        merged_ref[...] = jnp.zeros_like(merged_ref)
        bucket = bucket_ref[...]
        sink_col = lax.broadcasted_iota(jnp.int32, (WINDOW, 2 * WINDOW), 1) == 0
        for h in range(N_Q_HEADS):
            acc = jnp.zeros((WINDOW, 2 * WINDOW), jnp.float32)
            for bk in range(N_BUCKETS):
                acc = jnp.where(bucket == bk, rel_bias_ref[bk, h], acc)
            bias_ref[h] = jnp.where(sink_col, sinks_ref[layer, h], acc) * LOG2_E
        row = lax.broadcasted_iota(jnp.int32, (CHUNK, CHUNK), 0)
        col = lax.broadcasted_iota(jnp.int32, (CHUNK, CHUNK), 1)
        for g in range(SGU_GROUPS):
            wtril_ref[g] = jnp.where(col <= row, w_sp_ref[g], 0.0).astype(bf16)

    @pl.when(t_idx == 0)
    def _():
        for i in range(2 * N_KV_HEADS):
            kvar_ref[i, 0:WINDOW, :] = jnp.zeros((WINDOW, PAIR_WIDTH), bf16)
            vvar_ref[i, 0:WINDOW, :] = jnp.zeros((WINDOW, PAIR_WIDTH), bf16)

    out = _dot(merged_ref[...], w_o_ref[...])
    o_ref[0] = (xprev_ref[0]
                + out * lax.rsqrt(jnp.mean(out * out, axis=-1, keepdims=True) + EPS) * post_ref[...])

    x = x_ref[0]
    h_ref[...] = (x * lax.rsqrt(jnp.mean(x * x, axis=-1, keepdims=True) + EPS) * pre_ref[...]).astype(bf16)

    def proj(off, width):
        return _dot(h_ref[...], w_in_ref[:, off:off + width])

    q_ref[...] = (proj(Q_OFF, ATTN_WIDTH) * np.float32(HEAD_DIM ** -0.5 * LOG2_E)).astype(bf16)
    w_kv = w_in_ref[:, K_OFF:K_OFF + 2 * KV_WIDTH]
    kv = jnp.concatenate([_dot(h_ref[:TILE // 2, :], w_kv), _dot(h_ref[TILE // 2:, :], w_kv)], axis=0)
    low = lax.broadcasted_iota(jnp.int32, (TILE, PAIR_WIDTH), 1) < HEAD_DIM

    def place(var_ref, t):
        zero = jnp.zeros_like(t)
        swapped = pltpu.roll(t, HEAD_DIM, 1)
        var_ref[0, WINDOW:, :] = jnp.where(low, t, zero)
        var_ref[1, WINDOW:, :] = jnp.where(low, zero, swapped)
        var_ref[2, WINDOW:, :] = jnp.where(low, swapped, zero)
        var_ref[3, WINDOW:, :] = jnp.where(low, zero, t)

    place(kvar_ref, kv[:, :KV_WIDTH].astype(bf16))
    place(vvar_ref, kv[:, KV_WIDTH:].astype(bf16))

    qi = lax.broadcasted_iota(jnp.int32, (WINDOW, 2 * WINDOW), 0)
    kj = lax.broadcasted_iota(jnp.int32, (WINDOW, 2 * WINDOW), 1)
    dist = qi + WINDOW - kj
    in_window = jnp.logical_or(jnp.logical_and(dist >= 0, dist < WINDOW), kj == 0)
    first_mask = jnp.logical_and(in_window, jnp.logical_or(jnp.logical_or(kj >= WINDOW, kj == 0), t_idx > 0))
    band_low = lax.broadcasted_iota(jnp.int32, (2 * WINDOW, PAIR_WIDTH), 1) < HEAD_DIM
    ones_even = jnp.where(band_low, 1.0, 0.0).astype(bf16)
    ones_odd = jnp.where(band_low, 0.0, 1.0).astype(bf16)
    first_row = lax.broadcasted_iota(jnp.int32, (BF16_SUBLANES, PAIR_WIDTH), 0) == 0

    def band(var_ref, variant, r0):
        head = var_ref[variant, r0:r0 + BF16_SUBLANES, :]
        head = jnp.where(first_row, jnp.zeros_like(head), head)
        return jnp.concatenate([head, var_ref[variant, r0 + BF16_SUBLANES:r0 + 2 * WINDOW, :]], axis=0)

    def attn_scores(blk, j):
        r0 = blk * WINDOW
        q_g = jnp.concatenate(
            [q_ref[r0:r0 + WINDOW, (PAIRS_PER_KV * j + p) * PAIR_WIDTH:(PAIRS_PER_KV * j + p + 1) * PAIR_WIDTH]
             for p in range(PAIRS_PER_KV)], axis=0)
        k_g = jnp.concatenate([band(kvar_ref, 2 * j + e, r0) for e in range(2)], axis=0)
        return lax.dot_general(q_g, k_g, (((1,), (1,)), ((), ())),
                               preferred_element_type=jnp.float32)

    def attn_softmax(blk, j, s_g):
        mask = first_mask if blk == 0 else in_window
        rows = []
        for p in range(PAIRS_PER_KV):
            probs = []
            for e in range(2):
                hq = GQA_GROUP * j + 2 * p + e
                s = s_g[p * WINDOW:(p + 1) * WINDOW, e * 2 * WINDOW:(e + 1) * 2 * WINDOW]
                s = jnp.where(mask, s + bias_ref[hq], NEG_INF)
                m = jnp.max(s, axis=-1, keepdims=True)
                probs.append(jnp.exp2(s - m).astype(bf16))
            rows.append(jnp.concatenate(probs, axis=1))
        return jnp.concatenate(rows, axis=0)

    def attn_pv(blk, j, p_g):
        r0 = blk * WINDOW
        v_g = jnp.concatenate(
            [jnp.concatenate([band(vvar_ref, 2 * j + e, r0), ones], axis=1)
             for e, ones in ((0, ones_even), (1, ones_odd))], axis=0)
        r_g = _dot(p_g, v_g)
        for p in range(PAIRS_PER_KV):
            r = r_g[p * WINDOW:(p + 1) * WINDOW]
            c0 = (PAIRS_PER_KV * j + p) * PAIR_WIDTH
            o_pair = r[:, :PAIR_WIDTH] / r[:, PAIR_WIDTH:]
            yain_ref[r0:r0 + WINDOW, c0:c0 + PAIR_WIDTH] = (
                o_pair * ga_ref[r0:r0 + WINDOW, c0:c0 + PAIR_WIDTH]).astype(bf16)

    def spatial_mix():
        n_chunks = TILE // CHUNK
        for g in range(SGU_GROUPS):
            c_lo = g * SGU_GROUP_DIM
            v_g = jnp.concatenate(
                [vn_ref[c * CHUNK:(c + 1) * CHUNK, c_lo:c_lo + SGU_GROUP_DIM] for c in range(n_chunks)], axis=1)
            m_g = _dot(wtril_ref[g], v_g) + b_sp_ref[g]
            for c in range(n_chunks):
                rows = slice(c * CHUNK, (c + 1) * CHUNK)
                cols = slice(c_lo, c_lo + SGU_GROUP_DIM)
                ybin_ref[rows, cols] = (m_g[:, c * SGU_GROUP_DIM:(c + 1) * SGU_GROUP_DIM]
                                        * t_ref[rows, cols]).astype(bf16)

    def job_v():
        v_act = _gelu_exact(proj(VB_OFF, SGU_WIDTH))
        mu = jnp.mean(v_act, axis=-1, keepdims=True)
        vc = v_act - mu
        vn_ref[...] = (vc * lax.rsqrt(jnp.mean(vc * vc, axis=-1, keepdims=True) + EPS) * lng_ref[...]
                       + lnb_ref[...]).astype(bf16)

    def job_u():
        gu_ref[...] = _gelu_exact(proj(U_OFF, SGU_WIDTH)).astype(bf16)

    def job_gate_b():
        t_ref[...] = (_silu(proj(GB_OFF, SGU_WIDTH)) * gu_ref[...]).astype(bf16)

    def job_merge_b():
        sb_ref[...] = _sigmoid(proj(MB_OFF, D_MODEL)).astype(bf16)

    def job_branch_b():
        tb_ref[...] = _dot(ybin_ref[...], w_b_ref[...]) * sb_ref[...]

    def job_merge_a():
        sa_ref[...] = _sigmoid(proj(MA_OFF, D_MODEL)).astype(bf16)

    dense_jobs = [job_v, job_u, job_gate_b, spatial_mix, job_merge_b, job_branch_b, job_merge_a]
    groups = [(blk, j) for blk in range(BLOCKS_PER_TILE) for j in range(N_KV_HEADS)]
    ga_ref[...] = _silu(proj(GA_OFF, ATTN_WIDTH)).astype(bf16)
    s_g = attn_scores(*groups[0])
    for gi, group in enumerate(groups):
        if dense_jobs:
            dense_jobs.pop(0)()
        p_g = attn_softmax(*group, s_g)
        if gi + 1 < len(groups):
            s_g = attn_scores(*groups[gi + 1])
        attn_pv(*group, p_g)

    for i in range(2 * N_KV_HEADS):
        kvar_ref[i, 0:WINDOW, :] = kvar_ref[i, TILE:TILE + WINDOW, :]
        vvar_ref[i, 0:WINDOW, :] = vvar_ref[i, TILE:TILE + WINDOW, :]

    for job in dense_jobs:
        job()
    merged_ref[...] = (_dot(yain_ref[...], w_a_ref[...]) * sa_ref[...] + tb_ref[...]).astype(bf16)


def _resident(shape):
    return pl.BlockSpec(shape, lambda s, *_: (0,) * len(shape), pipeline_mode=pl.Buffered(1))


def _layer_slice(layer, shape):
    return pl.BlockSpec((None,) + shape, lambda s, *_: (layer,) + (0,) * len(shape),
                        pipeline_mode=pl.Buffered(1))


def _layer(layer, x, rel_bias, sinks, bucket, pre, post, lng, lnb, w_sp, b_sp, weights, next_weights):
    B, S, D = x.shape
    w_in, w_a, w_b, w_o = weights
    manual_weights = w_in.dtype == jnp.float32
    assert D == D_MODEL and S % TILE == 0 and w_in.shape[1:] == (D_MODEL, IN_COLS)
    tiles_per_seq = S // TILE
    n_tiles = B * tiles_per_seq
    cast_blocks = D_MODEL // CAST_ROWS
    assert n_tiles >= 2 * cast_blocks or not next_weights

    def tile_index(tile):
        return (tile // tiles_per_seq, tile % tiles_per_seq, 0)

    def cast_block(s):
        return jnp.minimum(s // (n_tiles // cast_blocks), cast_blocks - 1)

    x_spec = pl.BlockSpec((1, TILE, D), lambda s, *_: tile_index(jnp.minimum(s, n_tiles - 1)))
    o_spec = pl.BlockSpec((1, TILE, D), lambda s, *_: tile_index(jnp.maximum(s - 1, 0)))
    cast_in_specs = [pl.BlockSpec((None, CAST_ROWS, w.shape[2]), lambda s, *_: (layer + 1, cast_block(s), 0))
                     for w in next_weights]
    cast_out_specs = [pl.BlockSpec((None, CAST_ROWS, w.shape[2]), lambda s, *_: (0, cast_block(s), 0))
                      for w in next_weights]
    cast_out_shapes = [jax.ShapeDtypeStruct((1,) + w.shape[1:], jnp.bfloat16) for w in next_weights]
    stage_bf16 = pltpu.VMEM((TILE, D), jnp.bfloat16)
    stage_f32 = pltpu.VMEM((TILE, D), jnp.float32)
    weight_shapes = ((D, IN_COLS), (ATTN_WIDTH, D), (SGU_WIDTH, D), (D, D))
    if manual_weights:
        weight_specs = [pl.BlockSpec(memory_space=pl.ANY)] * 4
        weight_scratch = ([pltpu.VMEM(shape, jnp.bfloat16) for shape in weight_shapes]
                          + [pltpu.VMEM((STAGE_SLOTS, STAGE_ROWS, IN_COLS), jnp.float32),
                             pltpu.VMEM((STAGE_SLOTS, STAGE_ROWS_SQUARE, D), jnp.float32),
                             pltpu.SemaphoreType.DMA((STAGE_SLOTS,))])
    else:
        weight_specs = [_layer_slice(0, shape) for shape in weight_shapes]
        weight_scratch = []
    grid_spec = pltpu.PrefetchScalarGridSpec(
        num_scalar_prefetch=2,
        grid=(n_tiles + 1,),
        in_specs=[
            x_spec,
            o_spec,
            _resident((WINDOW, 2 * WINDOW)),
            _layer_slice(layer, (1, D)), _layer_slice(layer, (1, D)),
            _layer_slice(layer, (1, SGU_WIDTH)), _layer_slice(layer, (1, SGU_WIDTH)),
            weight_specs[0],
            _layer_slice(layer, (SGU_GROUPS, CHUNK, CHUNK)),
            _layer_slice(layer, (SGU_GROUPS, CHUNK, 1)),
            weight_specs[1], weight_specs[2], weight_specs[3],
        ] + cast_in_specs,
        out_specs=[o_spec] + cast_out_specs,
        scratch_shapes=[
            pltpu.VMEM((N_Q_HEADS, WINDOW, 2 * WINDOW), jnp.float32),
            pltpu.VMEM((SGU_GROUPS, CHUNK, CHUNK), jnp.bfloat16),
            pltpu.VMEM((2 * N_KV_HEADS, TILE + WINDOW, PAIR_WIDTH), jnp.bfloat16),
            pltpu.VMEM((2 * N_KV_HEADS, TILE + WINDOW, PAIR_WIDTH), jnp.bfloat16),
            stage_bf16,
            stage_bf16, stage_bf16,
            stage_bf16, stage_bf16,
            stage_bf16, stage_bf16, stage_bf16,
            stage_bf16, stage_bf16, stage_bf16,
            stage_f32,
        ] + weight_scratch,
    )
    outs = pl.pallas_call(
        functools.partial(_layer_kernel, tiles_per_seq=tiles_per_seq, layer=layer, n_cast=len(next_weights),
                          manual_weights=manual_weights),
        out_shape=[jax.ShapeDtypeStruct(x.shape, x.dtype)] + cast_out_shapes,
        grid_spec=grid_spec,
        compiler_params=pltpu.CompilerParams(
            dimension_semantics=("arbitrary",),
            vmem_limit_bytes=V7X_VMEM_LIMIT_BYTES),
        name="hybrid_layer",
    )(rel_bias, sinks, x, x, bucket, pre, post, lng, lnb, w_in, w_sp, b_sp, w_a, w_b, w_o, *next_weights)
    return outs[0], tuple(outs[1:])


def kernel(x, rel_bias, pre_norm, post_norm, w_in, sinks, sgu_ln_g, sgu_ln_b, w_spatial, b_spatial,
           w_branch_a, w_branch_b, w_out):
    depth = w_in.shape[0]
    bucket = jnp.asarray(_t5_bucket_table())
    small = (rel_bias, sinks, bucket,
             pre_norm[:, None], post_norm[:, None], sgu_ln_g[:, None], sgu_ln_b[:, None],
             w_spatial, b_spatial[..., None])
    stacked = (w_in, w_branch_a, w_branch_b, w_out)
    weights = stacked
    for l in range(depth):
        x, weights = _layer(l, x, *small, weights, stacked if l + 1 < depth else ())
    return x
```

```python
import functools

import jax
import jax.numpy as jnp
import numpy as np
from jax import lax
from jax.experimental import pallas as pl
from jax.experimental.pallas import tpu as pltpu

D_MODEL = 1024
HEAD_DIM = 64
N_Q_HEADS = 16
N_KV_HEADS = 2
GQA_GROUP = N_Q_HEADS // N_KV_HEADS
WINDOW = 128
ATTN_WIDTH = N_Q_HEADS * HEAD_DIM
KV_WIDTH = N_KV_HEADS * HEAD_DIM
PAIR_WIDTH = 2 * HEAD_DIM
PAIRS_PER_KV = GQA_GROUP // 2
N_BUCKETS = 32
MAX_DISTANCE = 128
SGU_WIDTH = D_MODEL
SGU_GROUPS = 8
SGU_GROUP_DIM = SGU_WIDTH // SGU_GROUPS
CHUNK = 128
EPS = 1e-6
NEG_INF = -1e30
LOG2_E = float(np.log2(np.e))
BF16_SUBLANES = 16
CAST_ROWS = BF16_SUBLANES
STAGE_ROWS = 64
STAGE_ROWS_SQUARE = 256
STAGE_SLOTS = 4

_SPLITS = (ATTN_WIDTH, KV_WIDTH, KV_WIDTH, ATTN_WIDTH, SGU_WIDTH, SGU_WIDTH, SGU_WIDTH, D_MODEL, D_MODEL)
_OFFS = tuple(int(o) for o in np.cumsum((0,) + _SPLITS))
IN_COLS = _OFFS[-1]
(Q_OFF, K_OFF, V_OFF, GA_OFF, U_OFF, VB_OFF, GB_OFF, MA_OFF, MB_OFF) = _OFFS[:-1]

TILE = 256
BLOCKS_PER_TILE = TILE // WINDOW
V7X_VMEM_LIMIT_BYTES = 56 * 1024 * 1024


def _t5_bucket_table():
    qi = np.arange(WINDOW)[:, None]
    kj = np.arange(2 * WINDOW)[None, :]
    dist = np.clip(qi + WINDOW - kj, 0, None)
    max_exact = N_BUCKETS // 2
    large = max_exact + (np.log(np.maximum(dist, max_exact) / max_exact)
                         / np.log(MAX_DISTANCE / max_exact)
                         * (N_BUCKETS - max_exact)).astype(np.int32)
    large = np.minimum(large, N_BUCKETS - 1)
    return np.where(dist < max_exact, dist, large).astype(np.int32)


def _dot(a, b):
    return jnp.dot(a, b, preferred_element_type=jnp.float32)


def _gelu_exact(x):
    return 0.5 * x * (1.0 + lax.erf(x * np.float32(np.sqrt(0.5))))


def _sigmoid(x):
    return 0.5 * (1.0 + jnp.tanh(0.5 * x))


def _silu(x):
    half = 0.5 * x
    return half * (1.0 + jnp.tanh(half))


def _load_cast(src_hbm, layer, dst_ref, stage_ref, sems):
    n_rows, n_cols = dst_ref.shape
    chunk_rows = stage_ref.shape[1]
    assert stage_ref.shape[2] == n_cols and n_rows % chunk_rows == 0
    n_chunks = n_rows // chunk_rows
    assert n_chunks >= STAGE_SLOTS - 1

    def copy(chunk, slot):
        return pltpu.make_async_copy(
            src_hbm.at[layer, pl.ds(pl.multiple_of(chunk * chunk_rows, chunk_rows), chunk_rows), :],
            stage_ref.at[slot], sems.at[slot])

    for chunk in range(STAGE_SLOTS - 1):
        copy(chunk, chunk).start()

    def body(chunk, carry):
        slot = chunk % STAGE_SLOTS
        ahead = chunk + STAGE_SLOTS - 1

        @pl.when(ahead < n_chunks)
        def _():
            copy(ahead, ahead % STAGE_SLOTS).start()

        copy(chunk, slot).wait()
        rows = pl.ds(pl.multiple_of(chunk * chunk_rows, chunk_rows), chunk_rows)
        dst_ref[rows, :] = stage_ref[slot].astype(jnp.bfloat16)
        return carry

    lax.fori_loop(0, n_chunks, body, 0)


def _layer_kernel(rel_bias_ref, sinks_ref, *refs, tiles_per_seq, layer, n_cast, manual_weights):
    (x_ref, xprev_ref, bucket_ref, pre_ref, post_ref, lng_ref, lnb_ref,
     w_in_ref, w_sp_ref, b_sp_ref, w_a_ref, w_b_ref, w_o_ref) = refs[:13]
    cast_src = refs[13:13 + n_cast]
    o_ref = refs[13 + n_cast]
    cast_dst = refs[14 + n_cast:14 + 2 * n_cast]
    (bias_ref, wtril_ref, kvar_ref, vvar_ref, merged_ref,
     h_ref, q_ref, ga_ref, yain_ref, vn_ref, gu_ref, t_ref, ybin_ref, sb_ref, sa_ref,
     tb_ref) = refs[14 + 2 * n_cast:30 + 2 * n_cast]
    step = pl.program_id(0)
    t_idx = jnp.minimum(step, pl.num_programs(0) - 2) % tiles_per_seq
    bf16 = jnp.bfloat16

    if manual_weights:
        hbm_weights = (w_in_ref, w_a_ref, w_b_ref, w_o_ref)
        w_in_ref, w_a_ref, w_b_ref, w_o_ref, stage_in_ref, stage_sq_ref, load_sems = refs[30 + 2 * n_cast:]

        @pl.when(step == 0)
        def _():
            stages = (stage_in_ref, stage_sq_ref, stage_sq_ref, stage_sq_ref)
            for src, dst, stage in zip(hbm_weights, (w_in_ref, w_a_ref, w_b_ref, w_o_ref), stages):
                _load_cast(src, layer, dst, stage, load_sems)

    for src, dst in zip(cast_src, cast_dst):
        dst[...] = src[...].astype(bf16)

    @pl.when(step == 0)
    def _():
        merged_ref[...] = jnp.zeros_like(merged_ref)
        bucket = bucket_ref[...]
        sink_col = lax.broadcasted_iota(jnp.int32, (WINDOW, 2 * WINDOW), 1) == 0
        for h in range(N_Q_HEADS):
            acc = jnp.zeros((WINDOW, 2 * WINDOW), jnp.float32)
            for bk in range(N_BUCKETS):
                acc = jnp.where(bucket == bk, rel_bias_ref[bk, h], acc)
            bias_ref[h] = jnp.where(sink_col, sinks_ref[layer, h], acc) * LOG2_E
        row = lax.broadcasted_iota(jnp.int32, (CHUNK, CHUNK), 0)
        col = lax.broadcasted_iota(jnp.int32, (CHUNK, CHUNK), 1)
        for g in range(SGU_GROUPS):
            wtril_ref[g] = jnp.where(col <= row, w_sp_ref[g], 0.0).astype(bf16)

    @pl.when(t_idx == 0)
    def _():
        for i in range(2 * N_KV_HEADS):
            kvar_ref[i, 0:WINDOW, :] = jnp.zeros((WINDOW, PAIR_WIDTH), bf16)
            vvar_ref[i, 0:WINDOW, :] = jnp.zeros((WINDOW, PAIR_WIDTH), bf16)

    out = _dot(merged_ref[...], w_o_ref[...])
    o_ref[0] = (xprev_ref[0]
                + out * lax.rsqrt(jnp.mean(out * out, axis=-1, keepdims=True) + EPS) * post_ref[layer:layer + 1, :])

    x = x_ref[0]
    h_ref[...] = (x * lax.rsqrt(jnp.mean(x * x, axis=-1, keepdims=True) + EPS)
                  * pre_ref[layer:layer + 1, :]).astype(bf16)

    def proj(off, width):
        return _dot(h_ref[...], w_in_ref[:, off:off + width])

    q_ref[...] = (proj(Q_OFF, ATTN_WIDTH) * np.float32(HEAD_DIM ** -0.5 * LOG2_E)).astype(bf16)
    w_kv = w_in_ref[:, K_OFF:K_OFF + 2 * KV_WIDTH]
    kv = jnp.concatenate([_dot(h_ref[:TILE // 2, :], w_kv), _dot(h_ref[TILE // 2:, :], w_kv)], axis=0)
    low = lax.broadcasted_iota(jnp.int32, (TILE, PAIR_WIDTH), 1) < HEAD_DIM

    def place(var_ref, t):
        zero = jnp.zeros_like(t)
        swapped = pltpu.roll(t, HEAD_DIM, 1)
        var_ref[0, WINDOW:, :] = jnp.where(low, t, zero)
        var_ref[1, WINDOW:, :] = jnp.where(low, zero, swapped)
        var_ref[2, WINDOW:, :] = jnp.where(low, swapped, zero)
        var_ref[3, WINDOW:, :] = jnp.where(low, zero, t)

    place(kvar_ref, kv[:, :KV_WIDTH].astype(bf16))
    place(vvar_ref, kv[:, KV_WIDTH:].astype(bf16))

    qi = lax.broadcasted_iota(jnp.int32, (WINDOW, 2 * WINDOW), 0)
    kj = lax.broadcasted_iota(jnp.int32, (WINDOW, 2 * WINDOW), 1)
    dist = qi + WINDOW - kj
    in_window = jnp.logical_or(jnp.logical_and(dist >= 0, dist < WINDOW), kj == 0)
    first_mask = jnp.logical_and(in_window, jnp.logical_or(jnp.logical_or(kj >= WINDOW, kj == 0), t_idx > 0))
    band_low = lax.broadcasted_iota(jnp.int32, (2 * WINDOW, PAIR_WIDTH), 1) < HEAD_DIM
    ones_even = jnp.where(band_low, 1.0, 0.0).astype(bf16)
    ones_odd = jnp.where(band_low, 0.0, 1.0).astype(bf16)
    first_row = lax.broadcasted_iota(jnp.int32, (BF16_SUBLANES, PAIR_WIDTH), 0) == 0

    def band(var_ref, variant, r0):
        head = var_ref[variant, r0:r0 + BF16_SUBLANES, :]
        head = jnp.where(first_row, jnp.zeros_like(head), head)
        return jnp.concatenate([head, var_ref[variant, r0 + BF16_SUBLANES:r0 + 2 * WINDOW, :]], axis=0)

    def attn_scores(blk, j):
        r0 = blk * WINDOW
        q_g = jnp.concatenate(
            [q_ref[r0:r0 + WINDOW, (PAIRS_PER_KV * j + p) * PAIR_WIDTH:(PAIRS_PER_KV * j + p + 1) * PAIR_WIDTH]
             for p in range(PAIRS_PER_KV)], axis=0)
        k_g = jnp.concatenate([band(kvar_ref, 2 * j + e, r0) for e in range(2)], axis=0)
        return lax.dot_general(q_g, k_g, (((1,), (1,)), ((), ())),
                               preferred_element_type=jnp.float32)

    def attn_softmax(blk, j, s_g):
        mask = first_mask if blk == 0 else in_window
        rows = []
        for p in range(PAIRS_PER_KV):
            probs = []
            for e in range(2):
                hq = GQA_GROUP * j + 2 * p + e
                s = s_g[p * WINDOW:(p + 1) * WINDOW, e * 2 * WINDOW:(e + 1) * 2 * WINDOW]
                s = jnp.where(mask, s + bias_ref[hq], NEG_INF)
                m = jnp.max(s, axis=-1, keepdims=True)
                probs.append(jnp.exp2(s - m).astype(bf16))
            rows.append(jnp.concatenate(probs, axis=1))
        return jnp.concatenate(rows, axis=0)

    def attn_pv(blk, j, p_g):
        r0 = blk * WINDOW
        v_g = jnp.concatenate(
            [jnp.concatenate([band(vvar_ref, 2 * j + e, r0), ones], axis=1)
             for e, ones in ((0, ones_even), (1, ones_odd))], axis=0)
        r_g = _dot(p_g, v_g)
        for p in range(PAIRS_PER_KV):
            r = r_g[p * WINDOW:(p + 1) * WINDOW]
            c0 = (PAIRS_PER_KV * j + p) * PAIR_WIDTH
            o_pair = r[:, :PAIR_WIDTH] / r[:, PAIR_WIDTH:]
            yain_ref[r0:r0 + WINDOW, c0:c0 + PAIR_WIDTH] = (
                o_pair * ga_ref[r0:r0 + WINDOW, c0:c0 + PAIR_WIDTH]).astype(bf16)

    def spatial_mix():
        n_chunks = TILE // CHUNK
        for g in range(SGU_GROUPS):
            c_lo = g * SGU_GROUP_DIM
            v_g = jnp.concatenate(
                [vn_ref[c * CHUNK:(c + 1) * CHUNK, c_lo:c_lo + SGU_GROUP_DIM] for c in range(n_chunks)], axis=1)
            m_g = _dot(wtril_ref[g], v_g) + b_sp_ref[g]
            for c in range(n_chunks):
                rows = slice(c * CHUNK, (c + 1) * CHUNK)
                cols = slice(c_lo, c_lo + SGU_GROUP_DIM)
                ybin_ref[rows, cols] = (m_g[:, c * SGU_GROUP_DIM:(c + 1) * SGU_GROUP_DIM]
                                        * t_ref[rows, cols]).astype(bf16)

    def job_v():
        v_act = _gelu_exact(proj(VB_OFF, SGU_WIDTH))
        mu = jnp.mean(v_act, axis=-1, keepdims=True)
        vc = v_act - mu
        vn_ref[...] = (vc * lax.rsqrt(jnp.mean(vc * vc, axis=-1, keepdims=True) + EPS)
                       * lng_ref[layer:layer + 1, :] + lnb_ref[layer:layer + 1, :]).astype(bf16)

    def job_u():
        gu_ref[...] = _gelu_exact(proj(U_OFF, SGU_WIDTH)).astype(bf16)

    def job_gate_b():
        t_ref[...] = (_silu(proj(GB_OFF, SGU_WIDTH)) * gu_ref[...]).astype(bf16)

    def job_merge_b():
        sb_ref[...] = _sigmoid(proj(MB_OFF, D_MODEL)).astype(bf16)

    def job_branch_b():
        tb_ref[...] = _dot(ybin_ref[...], w_b_ref[...]) * sb_ref[...]

    def job_merge_a():
        sa_ref[...] = _sigmoid(proj(MA_OFF, D_MODEL)).astype(bf16)

    dense_jobs = [job_v, job_u, job_gate_b, spatial_mix, job_merge_b, job_branch_b, job_merge_a]
    groups = [(blk, j) for blk in range(BLOCKS_PER_TILE) for j in range(N_KV_HEADS)]
    ga_ref[...] = _silu(proj(GA_OFF, ATTN_WIDTH)).astype(bf16)
    s_g = attn_scores(*groups[0])
    for gi, group in enumerate(groups):
        if dense_jobs:
            dense_jobs.pop(0)()
        p_g = attn_softmax(*group, s_g)
        if gi + 1 < len(groups):
            s_g = attn_scores(*groups[gi + 1])
        attn_pv(*group, p_g)

    for i in range(2 * N_KV_HEADS):
        kvar_ref[i, 0:WINDOW, :] = kvar_ref[i, TILE:TILE + WINDOW, :]
        vvar_ref[i, 0:WINDOW, :] = vvar_ref[i, TILE:TILE + WINDOW, :]

    for job in dense_jobs:
        job()
    merged_ref[...] = (_dot(yain_ref[...], w_a_ref[...]) * sa_ref[...] + tb_ref[...]).astype(bf16)


def _resident(shape):
    return pl.BlockSpec(shape, lambda s, *_: (0,) * len(shape), pipeline_mode=pl.Buffered(1))


def _layer_slice(layer, shape):
    return pl.BlockSpec((None,) + shape, lambda s, *_: (layer,) + (0,) * len(shape),
                        pipeline_mode=pl.Buffered(1))


def _layer(layer, x, rel_bias, sinks, bucket, pre, post, lng, lnb, w_sp, b_sp, weights, next_weights):
    B, S, D = x.shape
    w_in, w_a, w_b, w_o = weights
    manual_weights = w_in.dtype == jnp.float32
    assert D == D_MODEL and S % TILE == 0 and w_in.shape[1:] == (D_MODEL, IN_COLS)
    tiles_per_seq = S // TILE
    n_tiles = B * tiles_per_seq
    cast_blocks = D_MODEL // CAST_ROWS
    assert n_tiles >= 2 * cast_blocks or not next_weights

    def tile_index(tile):
        return (tile // tiles_per_seq, tile % tiles_per_seq, 0)

    def cast_block(s):
        return jnp.minimum(s // (n_tiles // cast_blocks), cast_blocks - 1)

    x_spec = pl.BlockSpec((1, TILE, D), lambda s, *_: tile_index(jnp.minimum(s, n_tiles - 1)))
    o_spec = pl.BlockSpec((1, TILE, D), lambda s, *_: tile_index(jnp.maximum(s - 1, 0)))
    cast_in_specs = [pl.BlockSpec((None, CAST_ROWS, w.shape[2]), lambda s, *_: (layer + 1, cast_block(s), 0))
                     for w in next_weights]
    cast_out_specs = [pl.BlockSpec((None, CAST_ROWS, w.shape[2]), lambda s, *_: (0, cast_block(s), 0))
                      for w in next_weights]
    cast_out_shapes = [jax.ShapeDtypeStruct((1,) + w.shape[1:], jnp.bfloat16) for w in next_weights]
    stage_bf16 = pltpu.VMEM((TILE, D), jnp.bfloat16)
    stage_f32 = pltpu.VMEM((TILE, D), jnp.float32)
    weight_shapes = ((D, IN_COLS), (ATTN_WIDTH, D), (SGU_WIDTH, D), (D, D))
    if manual_weights:
        weight_specs = [pl.BlockSpec(memory_space=pl.ANY)] * 4
        weight_scratch = ([pltpu.VMEM(shape, jnp.bfloat16) for shape in weight_shapes]
                          + [pltpu.VMEM((STAGE_SLOTS, STAGE_ROWS, IN_COLS), jnp.float32),
                             pltpu.VMEM((STAGE_SLOTS, STAGE_ROWS_SQUARE, D), jnp.float32),
                             pltpu.SemaphoreType.DMA((STAGE_SLOTS,))])
    else:
        weight_specs = [_layer_slice(0, shape) for shape in weight_shapes]
        weight_scratch = []
    grid_spec = pltpu.PrefetchScalarGridSpec(
        num_scalar_prefetch=2,
        grid=(n_tiles + 1,),
        in_specs=[
            x_spec,
            o_spec,
            _resident((WINDOW, 2 * WINDOW)),
            _resident(pre.shape), _resident(post.shape),
            _resident(lng.shape), _resident(lnb.shape),
            weight_specs[0],
            _layer_slice(layer, (SGU_GROUPS, CHUNK, CHUNK)),
            _layer_slice(layer, (SGU_GROUPS, CHUNK, 1)),
            weight_specs[1], weight_specs[2], weight_specs[3],
        ] + cast_in_specs,
        out_specs=[o_spec] + cast_out_specs,
        scratch_shapes=[
            pltpu.VMEM((N_Q_HEADS, WINDOW, 2 * WINDOW), jnp.float32),
            pltpu.VMEM((SGU_GROUPS, CHUNK, CHUNK), jnp.bfloat16),
            pltpu.VMEM((2 * N_KV_HEADS, TILE + WINDOW, PAIR_WIDTH), jnp.bfloat16),
            pltpu.VMEM((2 * N_KV_HEADS, TILE + WINDOW, PAIR_WIDTH), jnp.bfloat16),
            stage_bf16,
            stage_bf16, stage_bf16,
            stage_bf16, stage_bf16,
            stage_bf16, stage_bf16, stage_bf16,
            stage_bf16, stage_bf16, stage_bf16,
            stage_f32,
        ] + weight_scratch,
    )
    outs = pl.pallas_call(
        functools.partial(_layer_kernel, tiles_per_seq=tiles_per_seq, layer=layer, n_cast=len(next_weights),
                          manual_weights=manual_weights),
        out_shape=[jax.ShapeDtypeStruct(x.shape, x.dtype)] + cast_out_shapes,
        grid_spec=grid_spec,
        compiler_params=pltpu.CompilerParams(
            dimension_semantics=("arbitrary",),
            vmem_limit_bytes=V7X_VMEM_LIMIT_BYTES),
        name="hybrid_layer",
    )(rel_bias, sinks, x, x, bucket, pre, post, lng, lnb, w_in, w_sp, b_sp, w_a, w_b, w_o, *next_weights)
    return outs[0], tuple(outs[1:])


def kernel(x, rel_bias, pre_norm, post_norm, w_in, sinks, sgu_ln_g, sgu_ln_b, w_spatial, b_spatial,
           w_branch_a, w_branch_b, w_out):
    depth = w_in.shape[0]
    bucket = jnp.asarray(_t5_bucket_table())
    small = (rel_bias, sinks, bucket,
             pre_norm, post_norm, sgu_ln_g, sgu_ln_b,
             w_spatial, b_spatial[..., None])
    stacked = (w_in, w_branch_a, w_branch_b, w_out)
    weights = stacked
    for l in range(depth):
        x, weights = _layer(l, x, *small, weights, stacked if l + 1 < depth else ())
    return x
```

```python
import functools

import jax
import jax.numpy as jnp
import numpy as np
from jax import lax
from jax.experimental import pallas as pl
from jax.experimental.pallas import tpu as pltpu

D_MODEL = 1024
HEAD_DIM = 64
N_Q_HEADS = 16
N_KV_HEADS = 2
GQA_GROUP = N_Q_HEADS // N_KV_HEADS
WINDOW = 128
ATTN_WIDTH = N_Q_HEADS * HEAD_DIM
KV_WIDTH = N_KV_HEADS * HEAD_DIM
PAIR_WIDTH = 2 * HEAD_DIM
PAIRS_PER_KV = GQA_GROUP // 2
N_BUCKETS = 32
MAX_DISTANCE = 128
SGU_WIDTH = D_MODEL
SGU_GROUPS = 8
SGU_GROUP_DIM = SGU_WIDTH // SGU_GROUPS
CHUNK = 128
EPS = 1e-6
NEG_INF = -1e30
LOG2_E = float(np.log2(np.e))
BF16_SUBLANES = 16
CAST_ROWS = BF16_SUBLANES
STAGE_ROWS = 64
STAGE_ROWS_SQUARE = 256
STAGE_SLOTS = 4

_SPLITS = (ATTN_WIDTH, KV_WIDTH, KV_WIDTH, ATTN_WIDTH, SGU_WIDTH, SGU_WIDTH, SGU_WIDTH, D_MODEL, D_MODEL)
_OFFS = tuple(int(o) for o in np.cumsum((0,) + _SPLITS))
IN_COLS = _OFFS[-1]
(Q_OFF, K_OFF, V_OFF, GA_OFF, U_OFF, VB_OFF, GB_OFF, MA_OFF, MB_OFF) = _OFFS[:-1]

TILE = 256
BLOCKS_PER_TILE = TILE // WINDOW
V7X_VMEM_LIMIT_BYTES = 56 * 1024 * 1024


def _t5_bucket_table():
    qi = np.arange(WINDOW)[:, None]
    kj = np.arange(2 * WINDOW)[None, :]
    dist = np.clip(qi + WINDOW - kj, 0, None)
    max_exact = N_BUCKETS // 2
    large = max_exact + (np.log(np.maximum(dist, max_exact) / max_exact)
                         / np.log(MAX_DISTANCE / max_exact)
                         * (N_BUCKETS - max_exact)).astype(np.int32)
    large = np.minimum(large, N_BUCKETS - 1)
    return np.where(dist < max_exact, dist, large).astype(np.int32)


def _dot(a, b):
    return jnp.dot(a, b, preferred_element_type=jnp.float32)


def _gelu_exact(x):
    return 0.5 * x * (1.0 + lax.erf(x * np.float32(np.sqrt(0.5))))


def _sigmoid(x):
    return 0.5 * (1.0 + jnp.tanh(0.5 * x))


def _silu(x):
    half = 0.5 * x
    return half * (1.0 + jnp.tanh(half))


def _load_cast(src_hbm, layer, dst_ref, stage_ref, sems):
    n_rows, n_cols = dst_ref.shape
    chunk_rows = stage_ref.shape[1]
    assert stage_ref.shape[2] == n_cols and n_rows % chunk_rows == 0
    n_chunks = n_rows // chunk_rows
    assert n_chunks >= STAGE_SLOTS - 1

    def copy(chunk, slot):
        return pltpu.make_async_copy(
            src_hbm.at[layer, pl.ds(pl.multiple_of(chunk * chunk_rows, chunk_rows), chunk_rows), :],
            stage_ref.at[slot], sems.at[slot])

    for chunk in range(STAGE_SLOTS - 1):
        copy(chunk, chunk).start()

    def body(chunk, carry):
        slot = chunk % STAGE_SLOTS
        ahead = chunk + STAGE_SLOTS - 1

        @pl.when(ahead < n_chunks)
        def _():
            copy(ahead, ahead % STAGE_SLOTS).start()

        copy(chunk, slot).wait()
        rows = pl.ds(pl.multiple_of(chunk * chunk_rows, chunk_rows), chunk_rows)
        dst_ref[rows, :] = stage_ref[slot].astype(jnp.bfloat16)
        return carry

    lax.fori_loop(0, n_chunks, body, 0)


def _layer_kernel(rel_bias_ref, sinks_ref, *refs, tiles_per_seq, layer, n_cast, manual_weights):
    (x_ref, xprev_ref, bucket_ref, pre_ref, post_ref, lng_ref, lnb_ref,
     w_in_ref, w_sp_ref, b_sp_ref, w_a_ref, w_b_ref, w_o_ref) = refs[:13]
    cast_src = refs[13:13 + n_cast]
    o_ref = refs[13 + n_cast]
    cast_dst = refs[14 + n_cast:14 + 2 * n_cast]
    (bias_ref, wtril_ref, kvar_ref, vvar_ref, merged_ref,
     h_ref, q_ref, ga_ref, yain_ref, vn_ref, gu_ref, t_ref, ybin_ref, sb_ref, sa_ref,
     tb_ref) = refs[14 + 2 * n_cast:30 + 2 * n_cast]
    step = pl.program_id(0)
    t_idx = jnp.minimum(step, pl.num_programs(0) - 2) % tiles_per_seq
    bf16 = jnp.bfloat16

    if manual_weights:
        hbm_weights = (w_in_ref, w_a_ref, w_b_ref, w_o_ref)
        w_in_ref, w_a_ref, w_b_ref, w_o_ref, stage_in_ref, stage_sq_ref, load_sems = refs[30 + 2 * n_cast:]

        @pl.when(step == 0)
        def _():
            stages = (stage_in_ref, stage_sq_ref, stage_sq_ref, stage_sq_ref)
            for src, dst, stage in zip(hbm_weights, (w_in_ref, w_a_ref, w_b_ref, w_o_ref), stages):
                _load_cast(src, layer, dst, stage, load_sems)

    for src, dst in zip(cast_src, cast_dst):
        dst[...] = src[...].astype(bf16)

    @pl.when(step == 0)
    def _():
        merged_ref[...] = jnp.zeros_like(merged_ref)
        bucket = bucket_ref[...]
        sink_col = lax.broadcasted_iota(jnp.int32, (WINDOW, 2 * WINDOW), 1) == 0
        for h in range(N_Q_HEADS):
            acc = jnp.zeros((WINDOW, 2 * WINDOW), jnp.float32)
            for bk in range(N_BUCKETS):
                acc = jnp.where(bucket == bk, rel_bias_ref[bk, h], acc)
            bias_ref[h] = jnp.where(sink_col, sinks_ref[layer, h], acc) * LOG2_E
        row = lax.broadcasted_iota(jnp.int32, (CHUNK, CHUNK), 0)
        col = lax.broadcasted_iota(jnp.int32, (CHUNK, CHUNK), 1)
        for g in range(SGU_GROUPS):
            wtril_ref[g] = jnp.where(col <= row, w_sp_ref[g], 0.0).astype(bf16)

    @pl.when(t_idx == 0)
    def _():
        for i in range(2 * N_KV_HEADS):
            kvar_ref[i, 0:WINDOW, :] = jnp.zeros((WINDOW, PAIR_WIDTH), bf16)
            vvar_ref[i, 0:WINDOW, :] = jnp.zeros((WINDOW, PAIR_WIDTH), bf16)

    out = _dot(merged_ref[...], w_o_ref[...])
    o_ref[0] = (xprev_ref[0]
                + out * lax.rsqrt(jnp.mean(out * out, axis=-1, keepdims=True) + EPS) * post_ref[...])

    x = x_ref[0]
    h_ref[...] = (x * lax.rsqrt(jnp.mean(x * x, axis=-1, keepdims=True) + EPS) * pre_ref[...]).astype(bf16)

    def proj(off, width):
        return _dot(h_ref[...], w_in_ref[:, off:off + width])

    q_ref[...] = (proj(Q_OFF, ATTN_WIDTH) * np.float32(HEAD_DIM ** -0.5 * LOG2_E)).astype(bf16)
    w_kv = w_in_ref[:, K_OFF:K_OFF + 2 * KV_WIDTH]
    kv = jnp.concatenate([_dot(h_ref[:TILE // 2, :], w_kv), _dot(h_ref[TILE // 2:, :], w_kv)], axis=0)
    low = lax.broadcasted_iota(jnp.int32, (TILE, PAIR_WIDTH), 1) < HEAD_DIM

    def place(var_ref, t):
        zero = jnp.zeros_like(t)
        swapped = pltpu.roll(t, HEAD_DIM, 1)
        var_ref[0, WINDOW:, :] = jnp.where(low, t, zero)
        var_ref[1, WINDOW:, :] = jnp.where(low, zero, swapped)
        var_ref[2, WINDOW:, :] = jnp.where(low, swapped, zero)
        var_ref[3, WINDOW:, :] = jnp.where(low, zero, t)

    place(kvar_ref, kv[:, :KV_WIDTH].astype(bf16))
    place(vvar_ref, kv[:, KV_WIDTH:].astype(bf16))

    qi = lax.broadcasted_iota(jnp.int32, (WINDOW, 2 * WINDOW), 0)
    kj = lax.broadcasted_iota(jnp.int32, (WINDOW, 2 * WINDOW), 1)
    dist = qi + WINDOW - kj
    in_window = jnp.logical_or(jnp.logical_and(dist >= 0, dist < WINDOW), kj == 0)
    first_mask = jnp.logical_and(in_window, jnp.logical_or(jnp.logical_or(kj >= WINDOW, kj == 0), t_idx > 0))
    band_low = lax.broadcasted_iota(jnp.int32, (2 * WINDOW, PAIR_WIDTH), 1) < HEAD_DIM
    ones_even = jnp.where(band_low, 1.0, 0.0).astype(bf16)
    ones_odd = jnp.where(band_low, 0.0, 1.0).astype(bf16)
    first_row = lax.broadcasted_iota(jnp.int32, (BF16_SUBLANES, PAIR_WIDTH), 0) == 0

    def band(var_ref, variant, r0):
        head = var_ref[variant, r0:r0 + BF16_SUBLANES, :]
        head = jnp.where(first_row, jnp.zeros_like(head), head)
        return jnp.concatenate([head, var_ref[variant, r0 + BF16_SUBLANES:r0 + 2 * WINDOW, :]], axis=0)

    def attn_scores(blk, j):
        r0 = blk * WINDOW
        q_g = jnp.concatenate(
            [q_ref[r0:r0 + WINDOW, (PAIRS_PER_KV * j + p) * PAIR_WIDTH:(PAIRS_PER_KV * j + p + 1) * PAIR_WIDTH]
             for p in range(PAIRS_PER_KV)], axis=0)
        k_g = jnp.concatenate([band(kvar_ref, 2 * j + e, r0) for e in range(2)], axis=0)
        return lax.dot_general(q_g, k_g, (((1,), (1,)), ((), ())),
                               preferred_element_type=jnp.float32)

    def attn_softmax(blk, j, s_g):
        mask = first_mask if blk == 0 else in_window
        rows = []
        for p in range(PAIRS_PER_KV):
            probs = []
            for e in range(2):
                hq = GQA_GROUP * j + 2 * p + e
                s = s_g[p * WINDOW:(p + 1) * WINDOW, e * 2 * WINDOW:(e + 1) * 2 * WINDOW]
                s = jnp.where(mask, s + bias_ref[hq], NEG_INF)
                m = jnp.max(s, axis=-1, keepdims=True)
                probs.append(jnp.exp2(s - m).astype(bf16))
            rows.append(jnp.concatenate(probs, axis=1))
        return jnp.concatenate(rows, axis=0)

    def attn_pv(blk, j, p_g):
        r0 = blk * WINDOW
        v_g = jnp.concatenate(
            [jnp.concatenate([band(vvar_ref, 2 * j + e, r0), ones], axis=1)
             for e, ones in ((0, ones_even), (1, ones_odd))], axis=0)
        r_g = _dot(p_g, v_g)
        for p in range(PAIRS_PER_KV):
            r = r_g[p * WINDOW:(p + 1) * WINDOW]
            c0 = (PAIRS_PER_KV * j + p) * PAIR_WIDTH
            o_pair = r[:, :PAIR_WIDTH] / r[:, PAIR_WIDTH:]
            yain_ref[r0:r0 + WINDOW, c0:c0 + PAIR_WIDTH] = (
                o_pair * ga_ref[r0:r0 + WINDOW, c0:c0 + PAIR_WIDTH]).astype(bf16)

    def spatial_mix():
        n_chunks = TILE // CHUNK
        for g in range(SGU_GROUPS):
            c_lo = g * SGU_GROUP_DIM
            v_g = jnp.concatenate(
                [vn_ref[c * CHUNK:(c + 1) * CHUNK, c_lo:c_lo + SGU_GROUP_DIM] for c in range(n_chunks)], axis=1)
            m_g = _dot(wtril_ref[g], v_g) + b_sp_ref[g]
            for c in range(n_chunks):
                rows = slice(c * CHUNK, (c + 1) * CHUNK)
                cols = slice(c_lo, c_lo + SGU_GROUP_DIM)
                ybin_ref[rows, cols] = (m_g[:, c * SGU_GROUP_DIM:(c + 1) * SGU_GROUP_DIM]
                                        * t_ref[rows, cols]).astype(bf16)

    def job_v():
        v_act = _gelu_exact(proj(VB_OFF, SGU_WIDTH))
        mu = jnp.mean(v_act, axis=-1, keepdims=True)
        vc = v_act - mu
        vn_ref[...] = (vc * lax.rsqrt(jnp.mean(vc * vc, axis=-1, keepdims=True) + EPS) * lng_ref[...]
                       + lnb_ref[...]).astype(bf16)

    def job_u():
        gu_ref[...] = _gelu_exact(proj(U_OFF, SGU_WIDTH)).astype(bf16)

    def job_gate_b():
        t_ref[...] = (_silu(proj(GB_OFF, SGU_WIDTH)) * gu_ref[...]).astype(bf16)

    def job_merge_b():
        sb_ref[...] = _sigmoid(proj(MB_OFF, D_MODEL)).astype(bf16)

    def job_branch_b():
        tb_ref[...] = _dot(ybin_ref[...], w_b_ref[...]) * sb_ref[...]

    def job_merge_a():
        sa_ref[...] = _sigmoid(proj(MA_OFF, D_MODEL)).astype(bf16)

    dense_jobs = [job_v, job_u, job_gate_b, spatial_mix, job_merge_b, job_branch_b, job_merge_a]
    groups = [(blk, j) for blk in range(BLOCKS_PER_TILE) for j in range(N_KV_HEADS)]
    ga_ref[...] = _silu(proj(GA_OFF, ATTN_WIDTH)).astype(bf16)
    s_g = attn_scores(*groups[0])
    for gi, group in enumerate(groups):
        if dense_jobs:
            dense_jobs.pop(0)()
        p_g = attn_softmax(*group, s_g)
        if gi + 1 < len(groups):
            s_g = attn_scores(*groups[gi + 1])
        attn_pv(*group, p_g)

    for i in range(2 * N_KV_HEADS):
        kvar_ref[i, 0:WINDOW, :] = kvar_ref[i, TILE:TILE + WINDOW, :]
        vvar_ref[i, 0:WINDOW, :] = vvar_ref[i, TILE:TILE + WINDOW, :]

    for job in dense_jobs:
        job()
    merged_ref[...] = (_dot(yain_ref[...], w_a_ref[...]) * sa_ref[...] + tb_ref[...]).astype(bf16)


def _layer_call(rel_bias_ref, sinks_ref, x_hbm, bucket_ref, pre_ref, post_ref, lng_ref, lnb_ref,
                w_in_ref, w_sp_ref, b_sp_ref, w_a_ref, w_b_ref, w_o_ref, *rest,
                layer, n_cast, manual_weights, n_steps, tiled_in_specs, tiled_out_specs, tiles_per_seq):
    cast_src_hbm = rest[:n_cast]
    o_hbm = rest[n_cast]
    cast_dst_hbm = rest[n_cast + 1:2 * n_cast + 1]
    scratch = rest[2 * n_cast + 1:]
    if manual_weights:
        weights = (w_in_ref, w_a_ref, w_b_ref, w_o_ref)
    else:
        weights = tuple(w.at[0] for w in (w_in_ref, w_a_ref, w_b_ref, w_o_ref))
    resident = (bucket_ref, pre_ref.at[layer], post_ref.at[layer], lng_ref.at[layer], lnb_ref.at[layer],
                weights[0], w_sp_ref.at[layer], b_sp_ref.at[layer], weights[1], weights[2], weights[3])

    def step_body(x_ref, xprev_ref, *blocks):
        _layer_kernel(rel_bias_ref, sinks_ref, x_ref, xprev_ref, *resident, *blocks, *scratch,
                      tiles_per_seq=tiles_per_seq, layer=layer, n_cast=n_cast, manual_weights=manual_weights)

    pltpu.emit_pipeline(step_body, grid=(n_steps,), in_specs=tiled_in_specs, out_specs=tiled_out_specs)(
        x_hbm, x_hbm, *cast_src_hbm, o_hbm, *cast_dst_hbm)


def _layer(layer, x, rel_bias, sinks, bucket, pre, post, lng, lnb, w_sp, b_sp, weights, next_weights):
    B, S, D = x.shape
    w_in, w_a, w_b, w_o = weights
    manual_weights = w_in.dtype == jnp.float32
    assert D == D_MODEL and S % TILE == 0 and w_in.shape[1:] == (D_MODEL, IN_COLS)
    tiles_per_seq = S // TILE
    n_tiles = B * tiles_per_seq
    cast_blocks = D_MODEL // CAST_ROWS
    assert n_tiles >= 2 * cast_blocks or not next_weights

    def tile_index(tile):
        return (tile // tiles_per_seq, tile % tiles_per_seq, 0)

    def cast_block(s):
        return jnp.minimum(s // (n_tiles // cast_blocks), cast_blocks - 1)

    x_spec = pl.BlockSpec((1, TILE, D), lambda s, *_: tile_index(jnp.minimum(s, n_tiles - 1)))
    o_spec = pl.BlockSpec((1, TILE, D), lambda s, *_: tile_index(jnp.maximum(s - 1, 0)))
    cast_in_specs = [pl.BlockSpec((None, CAST_ROWS, w.shape[2]), lambda s, *_: (layer + 1, cast_block(s), 0))
                     for w in next_weights]
    cast_out_specs = [pl.BlockSpec((None, CAST_ROWS, w.shape[2]), lambda s, *_: (0, cast_block(s), 0))
                      for w in next_weights]
    cast_out_shapes = [jax.ShapeDtypeStruct((1,) + w.shape[1:], jnp.bfloat16) for w in next_weights]
    stage_bf16 = pltpu.VMEM((TILE, D), jnp.bfloat16)
    stage_f32 = pltpu.VMEM((TILE, D), jnp.float32)
    weight_shapes = ((D, IN_COLS), (ATTN_WIDTH, D), (SGU_WIDTH, D), (D, D))
    in_hbm = pl.BlockSpec(memory_space=pl.ANY)
    in_vmem = pl.BlockSpec(memory_space=pltpu.VMEM)
    in_smem = pl.BlockSpec(memory_space=pltpu.SMEM)
    if manual_weights:
        weight_spec = in_hbm
        weight_scratch = ([pltpu.VMEM(shape, jnp.bfloat16) for shape in weight_shapes]
                          + [pltpu.VMEM((STAGE_SLOTS, STAGE_ROWS, IN_COLS), jnp.float32),
                             pltpu.VMEM((STAGE_SLOTS, STAGE_ROWS_SQUARE, D), jnp.float32),
                             pltpu.SemaphoreType.DMA((STAGE_SLOTS,))])
    else:
        weight_spec = in_vmem
        weight_scratch = []
    outs = pl.pallas_call(
        functools.partial(_layer_call, layer=layer, n_cast=len(next_weights), manual_weights=manual_weights,
                          n_steps=n_tiles + 1, tiles_per_seq=tiles_per_seq,
                          tiled_in_specs=[x_spec, o_spec] + cast_in_specs,
                          tiled_out_specs=[o_spec] + cast_out_specs),
        out_shape=[jax.ShapeDtypeStruct(x.shape, x.dtype)] + cast_out_shapes,
        in_specs=[
            in_smem, in_smem,
            in_hbm,
            in_vmem,
            in_vmem, in_vmem, in_vmem, in_vmem,
            weight_spec, in_vmem, in_vmem, weight_spec, weight_spec, weight_spec,
        ] + [in_hbm] * len(next_weights),
        out_specs=[in_hbm] * (1 + len(next_weights)),
        scratch_shapes=[
            pltpu.VMEM((N_Q_HEADS, WINDOW, 2 * WINDOW), jnp.float32),
            pltpu.VMEM((SGU_GROUPS, CHUNK, CHUNK), jnp.bfloat16),
            pltpu.VMEM((2 * N_KV_HEADS, TILE + WINDOW, PAIR_WIDTH), jnp.bfloat16),
            pltpu.VMEM((2 * N_KV_HEADS, TILE + WINDOW, PAIR_WIDTH), jnp.bfloat16),
            stage_bf16,
            stage_bf16, stage_bf16,
            stage_bf16, stage_bf16,
            stage_bf16, stage_bf16, stage_bf16,
            stage_bf16, stage_bf16, stage_bf16,
            stage_f32,
        ] + weight_scratch,
        compiler_params=pltpu.CompilerParams(vmem_limit_bytes=V7X_VMEM_LIMIT_BYTES),
        name="hybrid_layer",
    )(rel_bias, sinks, x, bucket, pre, post, lng, lnb, w_in, w_sp, b_sp, w_a, w_b, w_o, *next_weights)
    return outs[0], tuple(outs[1:])


def kernel(x, rel_bias, pre_norm, post_norm, w_in, sinks, sgu_ln_g, sgu_ln_b, w_spatial, b_spatial,
           w_branch_a, w_branch_b, w_out):
    depth = w_in.shape[0]
    bucket = jnp.asarray(_t5_bucket_table())
    small = (rel_bias, sinks, bucket,
             pre_norm[:, None], post_norm[:, None], sgu_ln_g[:, None], sgu_ln_b[:, None],
             w_spatial, b_spatial[..., None])
    stacked = (w_in, w_branch_a, w_branch_b, w_out)
    weights = stacked
    for l in range(depth):
        x, weights = _layer(l, x, *small, weights, stacked if l + 1 < depth else ())
    return x
```

```python
import functools

import jax
import jax.numpy as jnp
import numpy as np
from jax import lax
from jax.experimental import pallas as pl
from jax.experimental.pallas import tpu as pltpu

D_MODEL = 1024
HEAD_DIM = 64
N_Q_HEADS = 16
N_KV_HEADS = 2
GQA_GROUP = N_Q_HEADS // N_KV_HEADS
WINDOW = 128
ATTN_WIDTH = N_Q_HEADS * HEAD_DIM
KV_WIDTH = N_KV_HEADS * HEAD_DIM
PAIR_WIDTH = 2 * HEAD_DIM
PAIRS_PER_KV = GQA_GROUP // 2
N_BUCKETS = 32
MAX_DISTANCE = 128
SGU_WIDTH = D_MODEL
SGU_GROUPS = 8
SGU_GROUP_DIM = SGU_WIDTH // SGU_GROUPS
CHUNK = 128
EPS = 1e-6
NEG_INF = -1e30
LOG2_E = float(np.log2(np.e))
BF16_SUBLANES = 16
CAST_ROWS = BF16_SUBLANES
STAGE_ROWS = 64
STAGE_ROWS_SQUARE = 256
STAGE_SLOTS = 4

_SPLITS = (ATTN_WIDTH, KV_WIDTH, KV_WIDTH, ATTN_WIDTH, SGU_WIDTH, SGU_WIDTH, SGU_WIDTH, D_MODEL, D_MODEL)
_OFFS = tuple(int(o) for o in np.cumsum((0,) + _SPLITS))
IN_COLS = _OFFS[-1]
(Q_OFF, K_OFF, V_OFF, GA_OFF, U_OFF, VB_OFF, GB_OFF, MA_OFF, MB_OFF) = _OFFS[:-1]

TILE = 256
BLOCKS_PER_TILE = TILE // WINDOW
V7X_VMEM_LIMIT_BYTES = 56 * 1024 * 1024


def _t5_bucket_table():
    qi = np.arange(WINDOW)[:, None]
    kj = np.arange(2 * WINDOW)[None, :]
    dist = np.clip(qi + WINDOW - kj, 0, None)
    max_exact = N_BUCKETS // 2
    large = max_exact + (np.log(np.maximum(dist, max_exact) / max_exact)
                         / np.log(MAX_DISTANCE / max_exact)
                         * (N_BUCKETS - max_exact)).astype(np.int32)
    large = np.minimum(large, N_BUCKETS - 1)
    return np.where(dist < max_exact, dist, large).astype(np.int32)


def _dot(a, b):
    return jnp.dot(a, b, preferred_element_type=jnp.float32)


def _gelu_exact(x):
    return 0.5 * x * (1.0 + lax.erf(x * np.float32(np.sqrt(0.5))))


def _sigmoid(x):
    return 0.5 * (1.0 + jnp.tanh(0.5 * x))


def _silu(x):
    half = 0.5 * x
    return half * (1.0 + jnp.tanh(half))


def _load_cast(src_hbm, layer, dst_ref, stage_ref, sems):
    n_rows, n_cols = dst_ref.shape
    chunk_rows = stage_ref.shape[1]
    assert stage_ref.shape[2] == n_cols and n_rows % chunk_rows == 0
    n_chunks = n_rows // chunk_rows
    assert n_chunks >= STAGE_SLOTS - 1

    def copy(chunk, slot):
        return pltpu.make_async_copy(
            src_hbm.at[layer, pl.ds(pl.multiple_of(chunk * chunk_rows, chunk_rows), chunk_rows), :],
            stage_ref.at[slot], sems.at[slot])

    for chunk in range(STAGE_SLOTS - 1):
        copy(chunk, chunk).start()

    def body(chunk, carry):
        slot = chunk % STAGE_SLOTS
        ahead = chunk + STAGE_SLOTS - 1

        @pl.when(ahead < n_chunks)
        def _():
            copy(ahead, ahead % STAGE_SLOTS).start()

        copy(chunk, slot).wait()
        rows = pl.ds(pl.multiple_of(chunk * chunk_rows, chunk_rows), chunk_rows)
        dst_ref[rows, :] = stage_ref[slot].astype(jnp.bfloat16)
        return carry

    lax.fori_loop(0, n_chunks, body, 0)


def _layer_kernel(rel_bias_ref, sinks_ref, *refs, tiles_per_seq, layer, n_cast, manual_weights, one_time):
    (x_ref, xprev_ref, bucket_ref, pre_ref, post_ref, lng_ref, lnb_ref,
     w_in_ref, w_sp_ref, b_sp_ref, w_a_ref, w_b_ref, w_o_ref) = refs[:13]
    cast_src = refs[13:13 + n_cast]
    o_ref = refs[13 + n_cast]
    cast_dst = refs[14 + n_cast:14 + 2 * n_cast]
    (bias_ref, wtril_ref, kvar_ref, vvar_ref, merged_ref,
     h_ref, q_ref, ga_ref, yain_ref, vn_ref, gu_ref, t_ref, ybin_ref, sb_ref, sa_ref,
     tb_ref) = refs[14 + 2 * n_cast:30 + 2 * n_cast]
    bf16 = jnp.bfloat16

    if manual_weights:
        hbm_weights = (w_in_ref, w_a_ref, w_b_ref, w_o_ref)
        w_in_ref, w_a_ref, w_b_ref, w_o_ref, stage_in_ref, stage_sq_ref, load_sems = refs[30 + 2 * n_cast:]

    if one_time:
        if manual_weights:
            stages = (stage_in_ref, stage_sq_ref, stage_sq_ref, stage_sq_ref)
            for src, dst, stage in zip(hbm_weights, (w_in_ref, w_a_ref, w_b_ref, w_o_ref), stages):
                _load_cast(src, layer, dst, stage, load_sems)
        merged_ref[...] = jnp.zeros_like(merged_ref)
        bucket = bucket_ref[...]
        sink_col = lax.broadcasted_iota(jnp.int32, (WINDOW, 2 * WINDOW), 1) == 0
        for h in range(N_Q_HEADS):
            acc = jnp.zeros((WINDOW, 2 * WINDOW), jnp.float32)
            for bk in range(N_BUCKETS):
                acc = jnp.where(bucket == bk, rel_bias_ref[bk, h], acc)
            bias_ref[h] = jnp.where(sink_col, sinks_ref[layer, h], acc) * LOG2_E
        row = lax.broadcasted_iota(jnp.int32, (CHUNK, CHUNK), 0)
        col = lax.broadcasted_iota(jnp.int32, (CHUNK, CHUNK), 1)
        for g in range(SGU_GROUPS):
            wtril_ref[g] = jnp.where(col <= row, w_sp_ref[g], 0.0).astype(bf16)
        return

    step = pl.program_id(0)
    t_idx = jnp.minimum(step, pl.num_programs(0) - 2) % tiles_per_seq
    for src, dst in zip(cast_src, cast_dst):
        dst[...] = src[...].astype(bf16)

    @pl.when(t_idx == 0)
    def _():
        for i in range(2 * N_KV_HEADS):
            kvar_ref[i, 0:WINDOW, :] = jnp.zeros((WINDOW, PAIR_WIDTH), bf16)
            vvar_ref[i, 0:WINDOW, :] = jnp.zeros((WINDOW, PAIR_WIDTH), bf16)

    out = _dot(merged_ref[...], w_o_ref[...])
    o_ref[0] = (xprev_ref[0]
                + out * lax.rsqrt(jnp.mean(out * out, axis=-1, keepdims=True) + EPS) * post_ref[...])

    x = x_ref[0]
    h_ref[...] = (x * lax.rsqrt(jnp.mean(x * x, axis=-1, keepdims=True) + EPS) * pre_ref[...]).astype(bf16)

    def proj(off, width):
        return _dot(h_ref[...], w_in_ref[:, off:off + width])

    q_ref[...] = (proj(Q_OFF, ATTN_WIDTH) * np.float32(HEAD_DIM ** -0.5 * LOG2_E)).astype(bf16)
    w_kv = w_in_ref[:, K_OFF:K_OFF + 2 * KV_WIDTH]
    kv = jnp.concatenate([_dot(h_ref[:TILE // 2, :], w_kv), _dot(h_ref[TILE // 2:, :], w_kv)], axis=0)
    low = lax.broadcasted_iota(jnp.int32, (TILE, PAIR_WIDTH), 1) < HEAD_DIM

    def place(var_ref, t):
        zero = jnp.zeros_like(t)
        swapped = pltpu.roll(t, HEAD_DIM, 1)
        var_ref[0, WINDOW:, :] = jnp.where(low, t, zero)
        var_ref[1, WINDOW:, :] = jnp.where(low, zero, swapped)
        var_ref[2, WINDOW:, :] = jnp.where(low, swapped, zero)
        var_ref[3, WINDOW:, :] = jnp.where(low, zero, t)

    place(kvar_ref, kv[:, :KV_WIDTH].astype(bf16))
    place(vvar_ref, kv[:, KV_WIDTH:].astype(bf16))

    qi = lax.broadcasted_iota(jnp.int32, (WINDOW, 2 * WINDOW), 0)
    kj = lax.broadcasted_iota(jnp.int32, (WINDOW, 2 * WINDOW), 1)
    dist = qi + WINDOW - kj
    in_window = jnp.logical_or(jnp.logical_and(dist >= 0, dist < WINDOW), kj == 0)
    first_mask = jnp.logical_and(in_window, jnp.logical_or(jnp.logical_or(kj >= WINDOW, kj == 0), t_idx > 0))
    band_low = lax.broadcasted_iota(jnp.int32, (2 * WINDOW, PAIR_WIDTH), 1) < HEAD_DIM
    ones_even = jnp.where(band_low, 1.0, 0.0).astype(bf16)
    ones_odd = jnp.where(band_low, 0.0, 1.0).astype(bf16)
    first_row = lax.broadcasted_iota(jnp.int32, (BF16_SUBLANES, PAIR_WIDTH), 0) == 0

    def band(var_ref, variant, r0):
        head = var_ref[variant, r0:r0 + BF16_SUBLANES, :]
        head = jnp.where(first_row, jnp.zeros_like(head), head)
        return jnp.concatenate([head, var_ref[variant, r0 + BF16_SUBLANES:r0 + 2 * WINDOW, :]], axis=0)

    def attn_scores(blk, j):
        r0 = blk * WINDOW
        q_g = jnp.concatenate(
            [q_ref[r0:r0 + WINDOW, (PAIRS_PER_KV * j + p) * PAIR_WIDTH:(PAIRS_PER_KV * j + p + 1) * PAIR_WIDTH]
             for p in range(PAIRS_PER_KV)], axis=0)
        k_g = jnp.concatenate([band(kvar_ref, 2 * j + e, r0) for e in range(2)], axis=0)
        return lax.dot_general(q_g, k_g, (((1,), (1,)), ((), ())),
                               preferred_element_type=jnp.float32)

    def attn_softmax(blk, j, s_g):
        mask = first_mask if blk == 0 else in_window
        rows = []
        for p in range(PAIRS_PER_KV):
            probs = []
            for e in range(2):
                hq = GQA_GROUP * j + 2 * p + e
                s = s_g[p * WINDOW:(p + 1) * WINDOW, e * 2 * WINDOW:(e + 1) * 2 * WINDOW]
                s = jnp.where(mask, s + bias_ref[hq], NEG_INF)
                m = jnp.max(s, axis=-1, keepdims=True)
                probs.append(jnp.exp2(s - m).astype(bf16))
            rows.append(jnp.concatenate(probs, axis=1))
        return jnp.concatenate(rows, axis=0)

    def attn_pv(blk, j, p_g):
        r0 = blk * WINDOW
        v_g = jnp.concatenate(
            [jnp.concatenate([band(vvar_ref, 2 * j + e, r0), ones], axis=1)
             for e, ones in ((0, ones_even), (1, ones_odd))], axis=0)
        r_g = _dot(p_g, v_g)
        for p in range(PAIRS_PER_KV):
            r = r_g[p * WINDOW:(p + 1) * WINDOW]
            c0 = (PAIRS_PER_KV * j + p) * PAIR_WIDTH
            o_pair = r[:, :PAIR_WIDTH] / r[:, PAIR_WIDTH:]
            yain_ref[r0:r0 + WINDOW, c0:c0 + PAIR_WIDTH] = (
                o_pair * ga_ref[r0:r0 + WINDOW, c0:c0 + PAIR_WIDTH]).astype(bf16)

    def spatial_mix():
        n_chunks = TILE // CHUNK
        for g in range(SGU_GROUPS):
            c_lo = g * SGU_GROUP_DIM
            v_g = jnp.concatenate(
                [vn_ref[c * CHUNK:(c + 1) * CHUNK, c_lo:c_lo + SGU_GROUP_DIM] for c in range(n_chunks)], axis=1)
            m_g = _dot(wtril_ref[g], v_g) + b_sp_ref[g]
            for c in range(n_chunks):
                rows = slice(c * CHUNK, (c + 1) * CHUNK)
                cols = slice(c_lo, c_lo + SGU_GROUP_DIM)
                ybin_ref[rows, cols] = (m_g[:, c * SGU_GROUP_DIM:(c + 1) * SGU_GROUP_DIM]
                                        * t_ref[rows, cols]).astype(bf16)

    def job_v():
        v_act = _gelu_exact(proj(VB_OFF, SGU_WIDTH))
        mu = jnp.mean(v_act, axis=-1, keepdims=True)
        vc = v_act - mu
        vn_ref[...] = (vc * lax.rsqrt(jnp.mean(vc * vc, axis=-1, keepdims=True) + EPS) * lng_ref[...]
                       + lnb_ref[...]).astype(bf16)

    def job_u():
        gu_ref[...] = _gelu_exact(proj(U_OFF, SGU_WIDTH)).astype(bf16)

    def job_gate_b():
        t_ref[...] = (_silu(proj(GB_OFF, SGU_WIDTH)) * gu_ref[...]).astype(bf16)

    def job_merge_b():
        sb_ref[...] = _sigmoid(proj(MB_OFF, D_MODEL)).astype(bf16)

    def job_branch_b():
        tb_ref[...] = _dot(ybin_ref[...], w_b_ref[...]) * sb_ref[...]

    def job_merge_a():
        sa_ref[...] = _sigmoid(proj(MA_OFF, D_MODEL)).astype(bf16)

    dense_jobs = [job_v, job_u, job_gate_b, spatial_mix, job_merge_b, job_branch_b, job_merge_a]
    groups = [(blk, j) for blk in range(BLOCKS_PER_TILE) for j in range(N_KV_HEADS)]
    ga_ref[...] = _silu(proj(GA_OFF, ATTN_WIDTH)).astype(bf16)
    s_g = attn_scores(*groups[0])
    for gi, group in enumerate(groups):
        if dense_jobs:
            dense_jobs.pop(0)()
        p_g = attn_softmax(*group, s_g)
        if gi + 1 < len(groups):
            s_g = attn_scores(*groups[gi + 1])
        attn_pv(*group, p_g)

    for i in range(2 * N_KV_HEADS):
        kvar_ref[i, 0:WINDOW, :] = kvar_ref[i, TILE:TILE + WINDOW, :]
        vvar_ref[i, 0:WINDOW, :] = vvar_ref[i, TILE:TILE + WINDOW, :]

    for job in dense_jobs:
        job()
    merged_ref[...] = (_dot(yain_ref[...], w_a_ref[...]) * sa_ref[...] + tb_ref[...]).astype(bf16)


def _layer_call(rel_bias_ref, sinks_ref, x_hbm, bucket_ref, pre_ref, post_ref, lng_ref, lnb_ref,
                w_in_ref, w_sp_ref, b_sp_ref, w_a_ref, w_b_ref, w_o_ref, *rest,
                layer, n_cast, manual_weights, n_steps, tiled_in_specs, tiled_out_specs, tiles_per_seq):
    cast_src_hbm = rest[:n_cast]
    o_hbm = rest[n_cast]
    cast_dst_hbm = rest[n_cast + 1:2 * n_cast + 1]
    scratch = rest[2 * n_cast + 1:]
    if manual_weights:
        weights = (w_in_ref, w_a_ref, w_b_ref, w_o_ref)
    else:
        weights = tuple(w.at[0] for w in (w_in_ref, w_a_ref, w_b_ref, w_o_ref))
    resident = (bucket_ref, pre_ref.at[layer], post_ref.at[layer], lng_ref.at[layer], lnb_ref.at[layer],
                weights[0], w_sp_ref.at[layer], b_sp_ref.at[layer], weights[1], weights[2], weights[3])

    layer_kernel = functools.partial(_layer_kernel, rel_bias_ref, sinks_ref, tiles_per_seq=tiles_per_seq,
                                     layer=layer, n_cast=n_cast, manual_weights=manual_weights)

    def step_body(x_ref, xprev_ref, *blocks):
        layer_kernel(x_ref, xprev_ref, *resident, *blocks, *scratch, one_time=False)

    no_blocks = (None,) * (2 * n_cast + 1)
    layer_kernel(None, None, *resident, *no_blocks, *scratch, one_time=True)
    pltpu.emit_pipeline(step_body, grid=(n_steps,), in_specs=tiled_in_specs, out_specs=tiled_out_specs)(
        x_hbm, x_hbm, *cast_src_hbm, o_hbm, *cast_dst_hbm)


def _layer(layer, x, rel_bias, sinks, bucket, pre, post, lng, lnb, w_sp, b_sp, weights, next_weights):
    B, S, D = x.shape
    w_in, w_a, w_b, w_o = weights
    manual_weights = w_in.dtype == jnp.float32
    assert D == D_MODEL and S % TILE == 0 and w_in.shape[1:] == (D_MODEL, IN_COLS)
    tiles_per_seq = S // TILE
    n_tiles = B * tiles_per_seq
    cast_blocks = D_MODEL // CAST_ROWS
    assert n_tiles >= 2 * cast_blocks or not next_weights

    def tile_index(tile):
        return (tile // tiles_per_seq, tile % tiles_per_seq, 0)

    def cast_block(s):
        return jnp.minimum(s // (n_tiles // cast_blocks), cast_blocks - 1)

    x_spec = pl.BlockSpec((1, TILE, D), lambda s, *_: tile_index(jnp.minimum(s, n_tiles - 1)))
    o_spec = pl.BlockSpec((1, TILE, D), lambda s, *_: tile_index(jnp.maximum(s - 1, 0)))
    cast_in_specs = [pl.BlockSpec((None, CAST_ROWS, w.shape[2]), lambda s, *_: (layer + 1, cast_block(s), 0))
                     for w in next_weights]
    cast_out_specs = [pl.BlockSpec((None, CAST_ROWS, w.shape[2]), lambda s, *_: (0, cast_block(s), 0))
                      for w in next_weights]
    cast_out_shapes = [jax.ShapeDtypeStruct((1,) + w.shape[1:], jnp.bfloat16) for w in next_weights]
    stage_bf16 = pltpu.VMEM((TILE, D), jnp.bfloat16)
    stage_f32 = pltpu.VMEM((TILE, D), jnp.float32)
    weight_shapes = ((D, IN_COLS), (ATTN_WIDTH, D), (SGU_WIDTH, D), (D, D))
    in_hbm = pl.BlockSpec(memory_space=pl.ANY)
    in_vmem = pl.BlockSpec(memory_space=pltpu.VMEM)
    in_smem = pl.BlockSpec(memory_space=pltpu.SMEM)
    if manual_weights:
        weight_spec = in_hbm
        weight_scratch = ([pltpu.VMEM(shape, jnp.bfloat16) for shape in weight_shapes]
                          + [pltpu.VMEM((STAGE_SLOTS, STAGE_ROWS, IN_COLS), jnp.float32),
                             pltpu.VMEM((STAGE_SLOTS, STAGE_ROWS_SQUARE, D), jnp.float32),
                             pltpu.SemaphoreType.DMA((STAGE_SLOTS,))])
    else:
        weight_spec = in_vmem
        weight_scratch = []
    outs = pl.pallas_call(
        functools.partial(_layer_call, layer=layer, n_cast=len(next_weights), manual_weights=manual_weights,
                          n_steps=n_tiles + 1, tiles_per_seq=tiles_per_seq,
                          tiled_in_specs=[x_spec, o_spec] + cast_in_specs,
                          tiled_out_specs=[o_spec] + cast_out_specs),
        out_shape=[jax.ShapeDtypeStruct(x.shape, x.dtype)] + cast_out_shapes,
        in_specs=[
            in_smem, in_smem,
            in_hbm,
            in_vmem,
            in_vmem, in_vmem, in_vmem, in_vmem,
            weight_spec, in_vmem, in_vmem, weight_spec, weight_spec, weight_spec,
        ] + [in_hbm] * len(next_weights),
        out_specs=[in_hbm] * (1 + len(next_weights)),
        scratch_shapes=[
            pltpu.VMEM((N_Q_HEADS, WINDOW, 2 * WINDOW), jnp.float32),
            pltpu.VMEM((SGU_GROUPS, CHUNK, CHUNK), jnp.bfloat16),
            pltpu.VMEM((2 * N_KV_HEADS, TILE + WINDOW, PAIR_WIDTH), jnp.bfloat16),
            pltpu.VMEM((2 * N_KV_HEADS, TILE + WINDOW, PAIR_WIDTH), jnp.bfloat16),
            stage_bf16,
            stage_bf16, stage_bf16,
            stage_bf16, stage_bf16,
            stage_bf16, stage_bf16, stage_bf16,
            stage_bf16, stage_bf16, stage_bf16,
            stage_f32,
        ] + weight_scratch,
        compiler_params=pltpu.CompilerParams(vmem_limit_bytes=V7X_VMEM_LIMIT_BYTES),
        name="hybrid_layer",
    )(rel_bias, sinks, x, bucket, pre, post, lng, lnb, w_in, w_sp, b_sp, w_a, w_b, w_o, *next_weights)
    return outs[0], tuple(outs[1:])


def kernel(x, rel_bias, pre_norm, post_norm, w_in, sinks, sgu_ln_g, sgu_ln_b, w_spatial, b_spatial,
           w_branch_a, w_branch_b, w_out):
    depth = w_in.shape[0]
    bucket = jnp.asarray(_t5_bucket_table())
    small = (rel_bias, sinks, bucket,
             pre_norm[:, None], post_norm[:, None], sgu_ln_g[:, None], sgu_ln_b[:, None],
             w_spatial, b_spatial[..., None])
    stacked = (w_in, w_branch_a, w_branch_b, w_out)
    weights = stacked
    for l in range(depth):
        x, weights = _layer(l, x, *small, weights, stacked if l + 1 < depth else ())
    return x
```
